```python
import math
import jax, jax.numpy as jnp
from jax import lax
import numpy as np

D_MODEL = 1024
BATCH = 32
SEQ = 2048
DEPTH = 1

MEM_LEN = 256
NORM_EPS = 1e-5

SSM_EXPAND = 2
SSM_D_INNER = SSM_EXPAND * D_MODEL
SSM_HEAD_DIM = 64
SSM_HEADS = SSM_D_INNER // SSM_HEAD_DIM
SSM_GROUPS = 8
SSM_STATE = 128
SSM_CONV = 4
SSM_CHUNK = 128
SSM_CONV_DIM = SSM_D_INNER + 2 * SSM_GROUPS * SSM_STATE

DIFF_HEADS = 8
DIFF_HEAD_DIM = 64
DIFF_V_DIM = 2 * DIFF_HEAD_DIM
DIFF_WIDTH = DIFF_HEADS * DIFF_V_DIM
Q_BLOCK = 128

REL_BUCKETS = 32
REL_MAX_DIST = 128

MEM_HEADS = 4
MEM_HEAD_DIM = 256
MEM_WIDTH = MEM_HEADS * MEM_HEAD_DIM

N_BRANCHES = 3

IN_SIZES = (
    SSM_D_INNER,
    SSM_CONV_DIM,
    SSM_HEADS,
    DIFF_WIDTH,
    DIFF_WIDTH,
    DIFF_WIDTH,
    DIFF_WIDTH,
    MEM_WIDTH,
    MEM_WIDTH,
    N_BRANCHES * D_MODEL,
)
IN_DIM = sum(IN_SIZES)
IN_SPLITS = [int(v) for v in np.cumsum(IN_SIZES)[:-1]]

kernel_name = "hybrid_ssd_diffattn_memxattn_gated"


def rms_norm(x, g):
    xf = x.astype(jnp.float32)
    y = xf * lax.rsqrt(jnp.mean(xf * xf, axis=-1, keepdims=True) + NORM_EPS)
    return (y * g.astype(jnp.float32)).astype(x.dtype)


def t5_bucket(rel):
    n = jnp.maximum(rel, 0)
    max_exact = REL_BUCKETS // 2
    nf = jnp.maximum(n, 1).astype(jnp.float32)
    large = max_exact + (jnp.log(nf / max_exact) / math.log(REL_MAX_DIST / max_exact)
                         * (REL_BUCKETS - max_exact)).astype(jnp.int32)
    large = jnp.minimum(large, REL_BUCKETS - 1)
    return jnp.where(n < max_exact, n, large)


def causal_dwconv(u, w, b):
    out = lax.conv_general_dilated(
        u, w[:, None, :], window_strides=(1,), padding=[(SSM_CONV - 1, 0)],
        dimension_numbers=('NWC', 'WIO', 'NWC'), feature_group_count=u.shape[-1])
    return out + b


def ssd_chunked(xs, dt, a, bm, cm):
    bsz, s = xs.shape[0], xs.shape[1]
    nc = s // SSM_CHUNK
    r = SSM_HEADS // SSM_GROUPS
    q = SSM_CHUNK
    xdt = (xs * dt[..., None]).reshape(bsz, nc, q, SSM_GROUPS, r, SSM_HEAD_DIM)
    adt = (dt * a).reshape(bsz, nc, q, SSM_GROUPS, r)
    bmc = bm.reshape(bsz, nc, q, SSM_GROUPS, SSM_STATE)
    cmc = cm.reshape(bsz, nc, q, SSM_GROUPS, SSM_STATE)
    xdt, adt, bmc, cmc = (jnp.moveaxis(t, 1, 0) for t in (xdt, adt, bmc, cmc))
    causal = jnp.tril(jnp.ones((q, q), dtype=bool))

    def step(state, inp):
        xc, ac, bc, cc = inp
        acs = jnp.cumsum(ac, axis=1)
        seg = acs[:, :, None] - acs[:, None, :]
        lmat = jnp.exp(jnp.where(causal[None, :, :, None, None], seg, -jnp.inf))
        cb = jnp.einsum('blgn,bsgn->blsg', cc, bc)
        y_diag = jnp.einsum('blsg,blsgr,bsgrp->blgrp', cb, lmat, xc)
        y_off = jnp.einsum('blgn,bgrpn,blgr->blgrp', cc, state, jnp.exp(acs))
        decay = jnp.exp(acs[:, -1:] - acs)
        new_state = (state * jnp.exp(acs[:, -1])[..., None, None]
                     + jnp.einsum('bsgn,bsgr,bsgrp->bgrpn', bc, decay, xc))
        return new_state.astype(state.dtype), (y_diag + y_off).astype(xc.dtype)

    state0 = jnp.zeros((bsz, SSM_GROUPS, r, SSM_HEAD_DIM, SSM_STATE), xdt.dtype)
    _, ys = lax.scan(step, state0, (xdt, adt, bmc, cmc))
    return jnp.moveaxis(ys, 0, 1).reshape(bsz, s, SSM_HEADS * SSM_HEAD_DIM)


def mamba_branch(z, xbc, dt_raw, conv_w, conv_b, dt_bias, a_log, d_skip, norm_g):
    bsz, s = z.shape[0], z.shape[1]
    xbc = jax.nn.silu(causal_dwconv(xbc, conv_w, conv_b))
    xs, bm, cm = jnp.split(xbc, [SSM_D_INNER, SSM_D_INNER + SSM_GROUPS * SSM_STATE], axis=-1)
    xs = xs.reshape(bsz, s, SSM_HEADS, SSM_HEAD_DIM)
    bm = bm.reshape(bsz, s, SSM_GROUPS, SSM_STATE)
    cm = cm.reshape(bsz, s, SSM_GROUPS, SSM_STATE)
    dt = jax.nn.softplus(dt_raw + dt_bias)
    a = -jnp.exp(a_log)
    y = ssd_chunked(xs, dt, a, bm, cm)
    y = y + (xs * d_skip[:, None]).reshape(bsz, s, SSM_D_INNER)
    y = y * jax.nn.silu(z)
    y = rms_norm(y.reshape(bsz, s, SSM_GROUPS, SSM_D_INNER // SSM_GROUPS),
                 norm_g.reshape(SSM_GROUPS, SSM_D_INNER // SSM_GROUPS))
    return y.reshape(bsz, s, SSM_D_INNER)


def diff_attn_branch(q, k, v, g, lq1, lk1, lq2, lk2, subln_g, rel_bias, lam_init):
    bsz, s = q.shape[0], q.shape[1]
    nb = s // Q_BLOCK
    q = q.reshape(bsz, s, DIFF_HEADS, 2, DIFF_HEAD_DIM).transpose(0, 2, 3, 1, 4)
    k = k.reshape(bsz, s, DIFF_HEADS, 2, DIFF_HEAD_DIM).transpose(0, 2, 3, 1, 4)
    v = v.reshape(bsz, s, DIFF_HEADS, DIFF_V_DIM).transpose(0, 2, 1, 3)
    lam = (jnp.exp(jnp.sum(lq1.astype(jnp.float32) * lk1.astype(jnp.float32)))
           - jnp.exp(jnp.sum(lq2.astype(jnp.float32) * lk2.astype(jnp.float32))) + lam_init)
    qb = jnp.moveaxis(q.reshape(bsz, DIFF_HEADS, 2, nb, Q_BLOCK, DIFF_HEAD_DIM), 3, 0)
    kpos = jnp.arange(s)
    scale = DIFF_HEAD_DIM ** -0.5

    def attend(args):
        qblk, blk = args
        qpos = blk * Q_BLOCK + jnp.arange(Q_BLOCK)
        rel = qpos[:, None] - kpos[None, :]
        bias = jnp.transpose(rel_bias[t5_bucket(rel)], (2, 0, 1)).astype(jnp.float32)
        sc = jnp.einsum('bhcqd,bhckd->bhcqk', qblk, k).astype(jnp.float32) * scale
        sc = jnp.where(rel >= 0, sc + bias[None, :, None], -jnp.inf)
        p = jax.nn.softmax(sc, axis=-1)
        amap = (p[:, :, 0] - lam * p[:, :, 1]).astype(v.dtype)
        return jnp.einsum('bhqk,bhkv->bhqv', amap, v)

    o = lax.map(attend, (qb, jnp.arange(nb)))
    o = o.transpose(1, 0, 3, 2, 4).reshape(bsz, s, DIFF_HEADS, DIFF_V_DIM)
    o = rms_norm(o, subln_g) * (1.0 - lam_init)
    return o.reshape(bsz, s, DIFF_WIDTH) * jax.nn.silu(g)


def mem_branch(q, g, mem, mem_norm_g, w_mem_kv):
    bsz, s = q.shape[0], q.shape[1]
    mem_n = rms_norm(mem, mem_norm_g)
    km, vm = jnp.split(mem_n @ w_mem_kv, 2, axis=-1)
    km = km.reshape(bsz, -1, MEM_HEADS, MEM_HEAD_DIM)
    vm = vm.reshape(bsz, -1, MEM_HEADS, MEM_HEAD_DIM)
    q = q.reshape(bsz, s, MEM_HEADS, MEM_HEAD_DIM)
    sc = jnp.einsum('bshd,bmhd->bhsm', q, km).astype(jnp.float32) * (MEM_HEAD_DIM ** -0.5)
    p = jax.nn.softmax(sc, axis=-1).astype(vm.dtype)
    o = jnp.einsum('bhsm,bmhd->bshd', p, vm).reshape(bsz, s, MEM_WIDTH)
    return o * jax.nn.silu(g)


def setup_inputs(seed: int = 0) -> dict:
    key = jax.random.key(seed)
    ks = jax.random.split(key, 24)
    nrm = jax.random.normal
    f32 = jnp.float32
    x = nrm(ks[0], (BATCH, SEQ, D_MODEL), f32)
    mem = nrm(ks[1], (BATCH, MEM_LEN, D_MODEL), f32)
    norm_gain = 1.0 + 0.02 * nrm(ks[2], (DEPTH, D_MODEL), f32)
    w_in = nrm(ks[3], (DEPTH, D_MODEL, IN_DIM), f32) * D_MODEL ** -0.5
    conv_w = nrm(ks[4], (DEPTH, SSM_CONV, SSM_CONV_DIM), f32) * SSM_CONV ** -0.5
    conv_b = 0.02 * nrm(ks[5], (DEPTH, SSM_CONV_DIM), f32)
    dt0 = jnp.exp(jax.random.uniform(ks[6], (DEPTH, SSM_HEADS), f32)
                  * (math.log(0.1) - math.log(0.001)) + math.log(0.001))
    dt_bias = dt0 + jnp.log(-jnp.expm1(-dt0))
    a_log = jnp.log(jax.random.uniform(ks[7], (DEPTH, SSM_HEADS), f32, minval=1.0, maxval=16.0))
    d_skip = 1.0 + 0.1 * nrm(ks[8], (DEPTH, SSM_HEADS), f32)
    ssm_norm_gain = 1.0 + 0.02 * nrm(ks[9], (DEPTH, SSM_D_INNER), f32)
    lambda_q1 = 0.1 * nrm(ks[10], (DEPTH, DIFF_HEAD_DIM), f32)
    lambda_k1 = 0.1 * nrm(ks[11], (DEPTH, DIFF_HEAD_DIM), f32)
    lambda_q2 = 0.1 * nrm(ks[12], (DEPTH, DIFF_HEAD_DIM), f32)
    lambda_k2 = 0.1 * nrm(ks[13], (DEPTH, DIFF_HEAD_DIM), f32)
    subln_gain = 1.0 + 0.02 * nrm(ks[14], (DEPTH, DIFF_V_DIM), f32)
    mem_norm_gain = 1.0 + 0.02 * nrm(ks[15], (DEPTH, D_MODEL), f32)
    w_mem_kv = nrm(ks[16], (DEPTH, D_MODEL, 2 * MEM_WIDTH), f32) * D_MODEL ** -0.5
    w_br_ssm = nrm(ks[17], (DEPTH, SSM_D_INNER, D_MODEL), f32) * SSM_D_INNER ** -0.5
    w_br_diff = nrm(ks[18], (DEPTH, DIFF_WIDTH, D_MODEL), f32) * DIFF_WIDTH ** -0.5
    w_br_mem = nrm(ks[19], (DEPTH, MEM_WIDTH, D_MODEL), f32) * MEM_WIDTH ** -0.5
    w_out = nrm(ks[20], (DEPTH, D_MODEL, D_MODEL), f32) * D_MODEL ** -0.5
    rel_bias = 0.5 * nrm(ks[21], (REL_BUCKETS, DIFF_HEADS), f32)
    final_norm_gain = 1.0 + 0.02 * nrm(ks[22], (D_MODEL,), f32)
    return {"x": x, "mem": mem, "norm_gain": norm_gain, "w_in": w_in,
            "conv_w": conv_w, "conv_b": conv_b, "dt_bias": dt_bias, "a_log": a_log,
            "d_skip": d_skip, "ssm_norm_gain": ssm_norm_gain,
            "lambda_q1": lambda_q1, "lambda_k1": lambda_k1,
            "lambda_q2": lambda_q2, "lambda_k2": lambda_k2,
            "subln_gain": subln_gain, "mem_norm_gain": mem_norm_gain,
            "w_mem_kv": w_mem_kv, "w_br_ssm": w_br_ssm, "w_br_diff": w_br_diff,
            "w_br_mem": w_br_mem, "w_out": w_out, "rel_bias": rel_bias,
            "final_norm_gain": final_norm_gain}


def reference(x, mem, norm_gain, w_in, conv_w, conv_b, dt_bias, a_log, d_skip,
              ssm_norm_gain, lambda_q1, lambda_k1, lambda_q2, lambda_k2, subln_gain,
              mem_norm_gain, w_mem_kv, w_br_ssm, w_br_diff, w_br_mem, w_out,
              rel_bias, final_norm_gain):
    bsz, s = x.shape[0], x.shape[1]
    for l in range(DEPTH):
        lam_init = 0.8 - 0.6 * math.exp(-0.3 * l)
        h = rms_norm(x, norm_gain[l])
        proj = h @ w_in[l]
        (z, xbc, dt_raw, dq, dk, dv, dg, mq, mg, gate_logits) = jnp.split(proj, IN_SPLITS, axis=-1)
        y_ssm = mamba_branch(z, xbc, dt_raw, conv_w[l], conv_b[l], dt_bias[l],
                             a_log[l], d_skip[l], ssm_norm_gain[l])
        y_diff = diff_attn_branch(dq, dk, dv, dg, lambda_q1[l], lambda_k1[l],
                                  lambda_q2[l], lambda_k2[l], subln_gain[l],
                                  rel_bias, lam_init)
        y_mem = mem_branch(mq, mg, mem, mem_norm_gain[l], w_mem_kv[l])
        gates = jax.nn.sigmoid(gate_logits).reshape(bsz, s, N_BRANCHES, D_MODEL)
        merged = (gates[:, :, 0] * (y_ssm @ w_br_ssm[l])
                  + gates[:, :, 1] * (y_diff @ w_br_diff[l])
                  + gates[:, :, 2] * (y_mem @ w_br_mem[l]))
        x = x + merged @ w_out[l]
    return rms_norm(x, final_norm_gain)
```

```python
import functools
import math

import numpy as np
import jax
import jax.numpy as jnp
from jax import lax
from jax.experimental import pallas as pl
from jax.experimental.pallas import tpu as pltpu

F32 = jnp.float32
BF16 = jnp.bfloat16

D_MODEL = 1024
NORM_EPS = 1e-5

SSM_D_INNER = 2048
SSM_HEAD_DIM = 64
SSM_HEADS = 32
SSM_GROUPS = 8
SSM_HEADS_PER_GROUP = SSM_HEADS // SSM_GROUPS
SSM_STATE = 128
SSM_CONV = 4
SSM_CHUNK = 128
SSM_BC_DIM = SSM_GROUPS * SSM_STATE
SSM_CONV_DIM = SSM_D_INNER + 2 * SSM_BC_DIM
SSM_GROUP_WIDTH = SSM_D_INNER // SSM_GROUPS

DIFF_HEADS = 8
DIFF_HEAD_DIM = 64
DIFF_V_DIM = 128
DIFF_WIDTH = 1024
ATT_BLOCK = 256

REL_BUCKETS = 32
REL_MAX_DIST = 128

MEM_HEADS = 4
MEM_HEAD_DIM = 256
MEM_WIDTH = 1024

N_BRANCHES = 3
LAM_INIT = 0.8 - 0.6 * math.exp(-0.3 * 0)

OFF_XBC = 0
OFF_Z = OFF_XBC + SSM_CONV_DIM
OFF_DQ = OFF_Z + SSM_D_INNER
OFF_DK = OFF_DQ + DIFF_WIDTH
OFF_DV = OFF_DK + DIFF_WIDTH
OFF_DG = OFF_DV + DIFF_WIDTH
OFF_MQ = OFF_DG + DIFF_WIDTH
OFF_MG = OFF_MQ + MEM_WIDTH
OFF_GATE = OFF_MG + MEM_WIDTH
PROJ_DIM = OFF_GATE + N_BRANCHES * D_MODEL
DT_PAD = 128

_REF_SIZES = (SSM_D_INNER, SSM_CONV_DIM, SSM_HEADS, DIFF_WIDTH, DIFF_WIDTH, DIFF_WIDTH,
              DIFF_WIDTH, MEM_WIDTH, MEM_WIDTH, N_BRANCHES * D_MODEL)
_REF_OFFS = [0] + [int(v) for v in np.cumsum(_REF_SIZES)]

VMEM_LIMIT_BYTES = 48 * 1024 * 1024


def _cparams(semantics):
    return pltpu.CompilerParams(dimension_semantics=semantics, vmem_limit_bytes=VMEM_LIMIT_BYTES)


def _dot(a, b):
    return jnp.dot(a, b, preferred_element_type=F32)


def _dot_nt(a, b):
    return lax.dot_general(a, b, (((1,), (1,)), ((), ())), preferred_element_type=F32)


def _dot_tn(a, b):
    return lax.dot_general(a, b, (((0,), (0,)), ((), ())), preferred_element_type=F32)


def _sigmoid(v):
    return 1.0 / (1.0 + jnp.exp(-v))


def _silu(v):
    return v * _sigmoid(v)


def _norm_matmul_kernel(x_ref, g_ref, w_ref, wdt_ref, o_ref, dt_ref, h_ref):
    @pl.when(pl.program_id(1) == 0)
    def _():
        x = x_ref[...]
        ms = jnp.mean(x * x, axis=-1, keepdims=True)
        h = (x * lax.rsqrt(ms + NORM_EPS) * g_ref[...]).astype(BF16)
        h_ref[...] = h
        dt_ref[...] = _dot(h, wdt_ref[...])

    o_ref[...] = _dot(h_ref[...], w_ref[...]).astype(o_ref.dtype)


def _norm_matmul(x2d, gain, w, w_small, tm, tn):
    t, d = x2d.shape
    n = w.shape[1]
    ns = w_small.shape[1]
    return pl.pallas_call(
        _norm_matmul_kernel,
        grid=(t // tm, n // tn),
        in_specs=[
            pl.BlockSpec((tm, d), lambda i, j: (i, 0)),
            pl.BlockSpec((1, d), lambda i, j: (0, 0)),
            pl.BlockSpec((d, tn), lambda i, j: (0, j)),
            pl.BlockSpec((d, ns), lambda i, j: (0, 0)),
        ],
        out_specs=[
            pl.BlockSpec((tm, tn), lambda i, j: (i, j)),
            pl.BlockSpec((tm, ns), lambda i, j: (i, 0)),
        ],
        out_shape=[jax.ShapeDtypeStruct((t, n), BF16), jax.ShapeDtypeStruct((t, ns), F32)],
        scratch_shapes=[pltpu.VMEM((tm, d), BF16)],
        compiler_params=_cparams(("parallel", "arbitrary")),
        name="norm_matmul",
    )(x2d, gain, w, w_small)


CONV_SLAB = 512
TAIL_ROWS = 8


def _cumsum_rows(v):
    n = v.shape[0]
    row = lax.broadcasted_iota(jnp.int32, v.shape, 0)
    shift = 1
    while shift < n:
        v = v + jnp.where(row >= shift, pltpu.roll(v, shift, axis=0), 0.0)
        shift *= 2
    return v


def _split_bf16(v):
    hi = v.astype(BF16)
    lo = (v - hi.astype(F32)).astype(BF16)
    return hi, lo


def _ssd_kernel(xbc_ref, z_ref, dtraw_ref, convw_ref, convb_ref, dtb_ref, alog_ref, dskip_ref,
                ng_ref, e_ref, y_ref, ubuf_ref, xs_ref, bc_ref, state_ref):
    q = SSM_CHUNK

    @pl.when(pl.program_id(1) == 0)
    def _():
        ubuf_ref[0:TAIL_ROWS, :] = jnp.zeros((TAIL_ROWS, SSM_CONV_DIM), F32)
        state_ref[...] = jnp.zeros_like(state_ref)

    for s in range(SSM_CONV_DIM // CONV_SLAB):
        c0 = s * CONV_SLAB
        cols = slice(c0, c0 + CONV_SLAB)
        u = xbc_ref[:, cols].astype(F32)
        ubuf_ref[TAIL_ROWS:TAIL_ROWS + q, cols] = u
        w = convw_ref[:, cols]
        acc = convb_ref[:, cols] + w[SSM_CONV - 1:SSM_CONV, :] * u
        for j in range(1, SSM_CONV):
            acc = acc + (w[SSM_CONV - 1 - j:SSM_CONV - j, :]
                         * ubuf_ref[TAIL_ROWS - j:TAIL_ROWS - j + q, cols])
        ubuf_ref[0:TAIL_ROWS, cols] = u[q - TAIL_ROWS:q, :]
        act = _silu(acc)
        if c0 < SSM_D_INNER:
            xs_ref[:, cols] = act
        else:
            bc_ref[:, c0 - SSM_D_INNER:c0 - SSM_D_INNER + CONV_SLAB] = act.astype(BF16)

    pre = dtraw_ref[...] + dtb_ref[...]
    dt = jnp.maximum(pre, 0.0) + jnp.log(1.0 + jnp.exp(-jnp.abs(pre)))
    a = -jnp.exp(alog_ref[...])
    acs = _cumsum_rows(dt * a)
    acs_t = acs.T
    eacs = jnp.exp(acs)
    dec = jnp.exp(acs[q - 1:q, :] - acs)
    dt_parts = _split_bf16(dt)
    eacs_parts = _split_bf16(eacs)
    dec_parts = _split_bf16(dec)

    row = lax.broadcasted_iota(jnp.int32, (q, q), 0)
    col = lax.broadcasted_iota(jnp.int32, (q, q), 1)
    causal = row >= col
    lane_head = lax.broadcasted_iota(jnp.int32, (q, SSM_GROUP_WIDTH), 1) // SSM_HEAD_DIM

    for g in range(SSM_GROUPS):
        lanes = slice(g * SSM_GROUP_WIDTH, (g + 1) * SSM_GROUP_WIDTH)
        e_g = e_ref[:, lanes]

        def expand(parts):
            return _dot(parts[0], e_g) + _dot(parts[1], e_g)

        dt_e = expand(dt_parts)
        eacs_e = expand(eacs_parts)
        dec_e = expand(dec_parts)

        xs_g = xs_ref[:, lanes]
        xdt = xs_g * dt_e
        xdt_b = xdt.astype(BF16)
        bm_g = bc_ref[:, g * SSM_STATE:(g + 1) * SSM_STATE]
        cm_g = bc_ref[:, SSM_BC_DIM + g * SSM_STATE:SSM_BC_DIM + (g + 1) * SSM_STATE]
        cb = _dot_nt(cm_g, bm_g)

        m_parts = []
        x_parts = []
        for r in range(SSM_HEADS_PER_GROUP):
            h = g * SSM_HEADS_PER_GROUP + r
            seg = acs[:, h:h + 1] - acs_t[h:h + 1, :]
            lmat = jnp.exp(jnp.where(causal, seg, -jnp.inf))
            m_parts.append((cb * lmat).astype(BF16))
            x_parts.append(jnp.where(lane_head == r, xdt_b, jnp.zeros_like(xdt_b)))
        m_cat = jnp.concatenate(m_parts, axis=1)
        x_bd = jnp.concatenate(x_parts, axis=0)
        y = _dot(m_cat, x_bd)

        st = state_ref[g]
        y = y + _dot(cm_g, st.astype(BF16)) * eacs_e
        xw = (xdt * dec_e).astype(BF16)
        state_ref[g] = st * eacs_e[q - 1:q, :] + _dot_tn(bm_g, xw)

        y = y + xs_g * dskip_ref[:, lanes]
        y = y * _silu(z_ref[:, lanes].astype(F32))
        ms = jnp.mean(y * y, axis=-1, keepdims=True)
        y = y * lax.rsqrt(ms + NORM_EPS) * ng_ref[:, lanes]
        y_ref[:, lanes] = y.astype(y_ref.dtype)


def _ssd(proj3, dt3, conv_w, conv_b, dt_bias, a_log, d_skip_e, norm_g, expand_mat):
    b, s, _ = proj3.shape
    q = SSM_CHUNK
    full = lambda shape: pl.BlockSpec(shape, lambda bi, ci: (0,) * len(shape))
    return pl.pallas_call(
        _ssd_kernel,
        grid=(b, s // q),
        in_specs=[
            pl.BlockSpec((None, q, SSM_CONV_DIM), lambda bi, ci: (bi, ci, OFF_XBC // SSM_CONV_DIM)),
            pl.BlockSpec((None, q, SSM_D_INNER), lambda bi, ci: (bi, ci, OFF_Z // SSM_D_INNER)),
            pl.BlockSpec((None, q, DT_PAD), lambda bi, ci: (bi, ci, 0)),
            full((SSM_CONV, SSM_CONV_DIM)),
            full((1, SSM_CONV_DIM)),
            full((1, DT_PAD)),
            full((1, DT_PAD)),
            full((1, SSM_D_INNER)),
            full((1, SSM_D_INNER)),
            full((DT_PAD, SSM_D_INNER)),
        ],
        out_specs=pl.BlockSpec((None, q, SSM_D_INNER), lambda bi, ci: (bi, ci, 0)),
        out_shape=jax.ShapeDtypeStruct((b, s, SSM_D_INNER), BF16),
        scratch_shapes=[
            pltpu.VMEM((TAIL_ROWS + q, SSM_CONV_DIM), F32),
            pltpu.VMEM((q, SSM_D_INNER), F32),
            pltpu.VMEM((q, 2 * SSM_BC_DIM), BF16),
            pltpu.VMEM((SSM_GROUPS, SSM_STATE, SSM_GROUP_WIDTH), F32),
        ],
        compiler_params=_cparams(("parallel", "arbitrary")),
        name="ssd",
    )(proj3, proj3, dt3, conv_w, conv_b, dt_bias, a_log, d_skip_e, norm_g, expand_mat)


def _t5_bucket_static(rel):
    n = np.maximum(rel, 0)
    max_exact = REL_BUCKETS // 2
    nf = np.maximum(n, 1).astype(np.float64)
    large = max_exact + (np.log(nf / max_exact) / math.log(REL_MAX_DIST / max_exact)
                         * (REL_BUCKETS - max_exact)).astype(np.int64)
    large = np.minimum(large, REL_BUCKETS - 1)
    return np.where(n < max_exact, n, large)


def _near_bucket_table():
    qi = np.arange(ATT_BLOCK)[:, None]
    kj = np.arange(2 * ATT_BLOCK)[None, :]
    rel = qi + ATT_BLOCK - kj
    return np.where(rel >= 0, _t5_bucket_static(rel), -1).astype(np.int32)


def _bias_table_kernel(relb_ref, bucket_ref, o_ref):
    h = pl.program_id(0)
    bucket = bucket_ref[...]
    acc = jnp.full(bucket.shape, -jnp.inf, F32)
    for bkt in range(REL_BUCKETS):
        acc = jnp.where(bucket == bkt, relb_ref[bkt, h], acc)
    o_ref[...] = acc


def _bias_table(rel_bias):
    bucket = jnp.asarray(_near_bucket_table())
    return pl.pallas_call(
        _bias_table_kernel,
        grid=(DIFF_HEADS,),
        in_specs=[
            pl.BlockSpec(memory_space=pltpu.SMEM),
            pl.BlockSpec(bucket.shape, lambda h: (0, 0)),
        ],
        out_specs=pl.BlockSpec((None,) + bucket.shape, lambda h: (h, 0, 0)),
        out_shape=jax.ShapeDtypeStruct((DIFF_HEADS,) + bucket.shape, F32),
        compiler_params=_cparams(("arbitrary",)),
        name="bias_table",
    )(rel_bias, bucket)


def _online_update(s, m, l, acc, v):
    m_new = jnp.maximum(m, jnp.max(s, axis=-1, keepdims=True))
    alpha = jnp.exp(m - m_new)
    p = jnp.exp(s - m_new)
    l_new = alpha * l + jnp.sum(p, axis=-1, keepdims=True)
    acc_new = alpha * acc + _dot(p.astype(BF16), v)
    return m_new, l_new, acc_new


def _diff_attn_kernel(farb_ref, q_ref, k_ref, v_ref, g_ref, tab_ref, lam_ref, sg_ref, o_ref):
    hd = pl.program_id(1)
    i = pl.program_id(2)
    blk = ATT_BLOCK

    qq = q_ref[...]
    lane = lax.broadcasted_iota(jnp.int32, qq.shape, 1)
    scale = DIFF_HEAD_DIM ** -0.5
    zero = jnp.zeros_like(qq)
    q1 = jnp.where(lane < DIFF_HEAD_DIM, qq, zero) * scale
    q2 = jnp.where(lane >= DIFF_HEAD_DIM, qq, zero) * scale

    far_bias = farb_ref[hd]

    def init():
        return (jnp.full((blk, 1), -jnp.inf, F32), jnp.zeros((blk, 1), F32),
                jnp.zeros((blk, DIFF_V_DIM), F32))

    def far_body(j, carry):
        c1, c2 = carry
        start = pl.multiple_of(j * blk, blk)
        kb = k_ref[pl.ds(start, blk), :]
        vb = v_ref[pl.ds(start, blk), :]
        c1 = _online_update(_dot_nt(q1, kb) + far_bias, *c1, vb)
        c2 = _online_update(_dot_nt(q2, kb) + far_bias, *c2, vb)
        return c1, c2

    c1, c2 = lax.fori_loop(0, jnp.maximum(i - 1, 0), far_body, (init(), init()))

    prev = pl.multiple_of(jnp.maximum(i - 1, 0) * blk, blk)
    diag = pl.multiple_of(i * blk, blk)
    kn = jnp.concatenate([k_ref[pl.ds(prev, blk), :], k_ref[pl.ds(diag, blk), :]], axis=0)
    vn = jnp.concatenate([v_ref[pl.ds(prev, blk), :], v_ref[pl.ds(diag, blk), :]], axis=0)
    coln = lax.broadcasted_iota(jnp.int32, (blk, 2 * blk), 1)
    no_prev = jnp.where(i == 0, blk, 0)
    bias = jnp.where(coln < no_prev, -jnp.inf, tab_ref[...])
    c1 = _online_update(_dot_nt(q1, kn) + bias, *c1, vn)
    c2 = _online_update(_dot_nt(q2, kn) + bias, *c2, vn)

    lam_p = lam_ref[...]
    lam = (jnp.exp(jnp.sum(lam_p[0:1, :] * lam_p[1:2, :], axis=-1, keepdims=True))
           - jnp.exp(jnp.sum(lam_p[2:3, :] * lam_p[3:4, :], axis=-1, keepdims=True)) + LAM_INIT)
    o = c1[2] / c1[1] - lam * (c2[2] / c2[1])
    ms = jnp.mean(o * o, axis=-1, keepdims=True)
    o = o * lax.rsqrt(ms + NORM_EPS) * sg_ref[...] * (1.0 - LAM_INIT)
    o_ref[...] = (o * _silu(g_ref[...].astype(F32))).astype(o_ref.dtype)


def _diff_attn(proj3, table, far_bias, lam_params, subln_g):
    b, s, _ = proj3.shape
    blk = ATT_BLOCK
    hw = DIFF_V_DIM
    return pl.pallas_call(
        _diff_attn_kernel,
        grid=(b, DIFF_HEADS, s // blk),
        in_specs=[
            pl.BlockSpec(memory_space=pltpu.SMEM),
            pl.BlockSpec((None, blk, hw), lambda bi, h, i: (bi, i, OFF_DQ // hw + h)),
            pl.BlockSpec((None, s, hw), lambda bi, h, i: (bi, 0, OFF_DK // hw + h)),
            pl.BlockSpec((None, s, hw), lambda bi, h, i: (bi, 0, OFF_DV // hw + h)),
            pl.BlockSpec((None, blk, hw), lambda bi, h, i: (bi, i, OFF_DG // hw + h)),
            pl.BlockSpec((None, blk, 2 * blk), lambda bi, h, i: (h, 0, 0)),
            pl.BlockSpec((4, DIFF_HEAD_DIM), lambda bi, h, i: (0, 0)),
            pl.BlockSpec((1, hw), lambda bi, h, i: (0, 0)),
        ],
        out_specs=pl.BlockSpec((None, blk, hw), lambda bi, h, i: (bi, i, h)),
        out_shape=jax.ShapeDtypeStruct((b, s, DIFF_WIDTH), BF16),
        compiler_params=_cparams(("parallel", "parallel", "arbitrary")),
        name="diff_attn",
    )(far_bias, proj3, proj3, proj3, proj3, table, lam_params, subln_g)


def _mem_attn_kernel(q_ref, g_ref, k_ref, v_ref, o_ref):
    scale = MEM_HEAD_DIM ** -0.5
    s = _dot_nt(q_ref[...] * scale, k_ref[...])
    m = jnp.max(s, axis=-1, keepdims=True)
    p = jnp.exp(s - m)
    l = jnp.sum(p, axis=-1, keepdims=True)
    o = _dot(p.astype(BF16), v_ref[...]) / l
    o_ref[...] = (o * _silu(g_ref[...].astype(F32))).astype(o_ref.dtype)


def _mem_attn(proj3, kv3, tq):
    b, s, _ = proj3.shape
    m = kv3.shape[1]
    hw = MEM_HEAD_DIM
    return pl.pallas_call(
        _mem_attn_kernel,
        grid=(b, MEM_HEADS, s // tq),
        in_specs=[
            pl.BlockSpec((None, tq, hw), lambda bi, h, i: (bi, i, OFF_MQ // hw + h)),
            pl.BlockSpec((None, tq, hw), lambda bi, h, i: (bi, i, OFF_MG // hw + h)),
            pl.BlockSpec((None, m, hw), lambda bi, h, i: (bi, 0, h)),
            pl.BlockSpec((None, m, hw), lambda bi, h, i: (bi, 0, MEM_HEADS + h)),
        ],
        out_specs=pl.BlockSpec((None, tq, hw), lambda bi, h, i: (bi, i, h)),
        out_shape=jax.ShapeDtypeStruct((b, s, MEM_WIDTH), BF16),
        compiler_params=_cparams(("parallel", "parallel", "parallel")),
        name="mem_attn",
    )(proj3, proj3, kv3, kv3)


def _merge_kernel(x_ref, ys_ref, yd_ref, ym_ref, gate_ref, ws_ref, wd_ref, wm_ref, wo_ref,
                  fg_ref, o_ref):
    d = D_MODEL
    gate = gate_ref[...].astype(F32)
    merged = (_sigmoid(gate[:, 0:d]) * _dot(ys_ref[...], ws_ref[...])
              + _sigmoid(gate[:, d:2 * d]) * _dot(yd_ref[...], wd_ref[...])
              + _sigmoid(gate[:, 2 * d:3 * d]) * _dot(ym_ref[...], wm_ref[...]))
    xo = x_ref[...] + _dot(merged.astype(BF16), wo_ref[...])
    ms = jnp.mean(xo * xo, axis=-1, keepdims=True)
    o_ref[...] = xo * lax.rsqrt(ms + NORM_EPS) * fg_ref[...]


def _merge(x2d, ys, yd, ym, proj, w_s, w_d, w_m, w_o, final_g, tm):
    t, d = x2d.shape
    gw = N_BRANCHES * D_MODEL
    rows = lambda width, cb=0: pl.BlockSpec((tm, width), lambda i: (i, cb))
    whole = lambda shape: pl.BlockSpec(shape, lambda i: (0, 0))
    return pl.pallas_call(
        _merge_kernel,
        grid=(t // tm,),
        in_specs=[
            rows(d), rows(SSM_D_INNER), rows(DIFF_WIDTH), rows(MEM_WIDTH),
            rows(gw, OFF_GATE // gw),
            whole(w_s.shape), whole(w_d.shape), whole(w_m.shape), whole(w_o.shape),
            whole((1, d)),
        ],
        out_specs=rows(d),
        out_shape=jax.ShapeDtypeStruct((t, d), F32),
        compiler_params=_cparams(("parallel",)),
        name="merge",
    )(x2d, ys, yd, ym, proj, w_s, w_d, w_m, w_o, final_g)


def _head_expand_matrix():
    e = np.zeros((DT_PAD, SSM_D_INNER), np.float32)
    for h in range(SSM_HEADS):
        e[h, h * SSM_HEAD_DIM:(h + 1) * SSM_HEAD_DIM] = 1.0
    return e


def _pad_lanes(v, width):
    return jnp.pad(v.reshape(1, -1), ((0, 0), (0, width - v.shape[-1])))


def kernel(x, mem, norm_gain, w_in, conv_w, conv_b, dt_bias, a_log, d_skip, ssm_norm_gain,
           lambda_q1, lambda_k1, lambda_q2, lambda_k2, subln_gain, mem_norm_gain, w_mem_kv,
           w_br_ssm, w_br_diff, w_br_mem, w_out, rel_bias, final_norm_gain):
    b, s, d = x.shape
    t = b * s
    assert norm_gain.shape[0] == 1, "single-layer (DEPTH == 1) kernel"
    assert s % ATT_BLOCK == 0 and s % SSM_CHUNK == 0 and d == D_MODEL
    assert int(_t5_bucket_static(np.array([ATT_BLOCK + 1]))[0]) == REL_BUCKETS - 1

    ro = _REF_OFFS
    wi = w_in[0]
    w_main = jnp.concatenate([
        wi[:, ro[1]:ro[2]],
        wi[:, ro[0]:ro[1]],
        wi[:, ro[3]:ro[10]],
    ], axis=1).astype(BF16)
    w_dt = jnp.pad(wi[:, ro[2]:ro[3]], ((0, 0), (0, DT_PAD - SSM_HEADS))).astype(BF16)

    x2d = x.reshape(t, d)
    tm = min(1024, t)
    proj, dt_raw = _norm_matmul(x2d, norm_gain[0].reshape(1, d), w_main, w_dt, tm, 1024)
    proj3 = proj.reshape(b, s, PROJ_DIM)
    dt3 = dt_raw.reshape(b, s, DT_PAD)

    y_ssm = _ssd(
        proj3, dt3, conv_w[0], conv_b[0].reshape(1, -1),
        _pad_lanes(dt_bias[0], DT_PAD), _pad_lanes(a_log[0], DT_PAD),
        jnp.repeat(d_skip[0], SSM_HEAD_DIM).reshape(1, -1), ssm_norm_gain[0].reshape(1, -1),
        jnp.asarray(_head_expand_matrix(), BF16))

    table = _bias_table(rel_bias)
    lam_params = jnp.stack([lambda_q1[0], lambda_k1[0], lambda_q2[0], lambda_k2[0]])
    y_diff = _diff_attn(proj3, table, rel_bias[REL_BUCKETS - 1], lam_params,
                        subln_gain[0].reshape(1, -1))

    m_len = mem.shape[1]
    kv_pad = jnp.zeros((d, DT_PAD), BF16)
    kv, _ = _norm_matmul(mem.reshape(b * m_len, d), mem_norm_gain[0].reshape(1, d),
                         w_mem_kv[0].astype(BF16), kv_pad, min(1024, b * m_len), 1024)
    y_mem = _mem_attn(proj3, kv.reshape(b, m_len, 2 * MEM_WIDTH), min(1024, s))

    out = _merge(x2d, y_ssm.reshape(t, -1), y_diff.reshape(t, -1), y_mem.reshape(t, -1), proj,
                 w_br_ssm[0].astype(BF16), w_br_diff[0].astype(BF16), w_br_mem[0].astype(BF16),
                 w_out[0].astype(BF16), final_norm_gain.reshape(1, d), min(512, t))
    return out.reshape(b, s, d)
```

```python
import functools
import math

import numpy as np
import jax
import jax.numpy as jnp
from jax import lax
from jax.experimental import pallas as pl
from jax.experimental.pallas import tpu as pltpu

F32 = jnp.float32
BF16 = jnp.bfloat16

D_MODEL = 1024
NORM_EPS = 1e-5

SSM_D_INNER = 2048
SSM_HEAD_DIM = 64
SSM_HEADS = 32
SSM_GROUPS = 8
SSM_HEADS_PER_GROUP = SSM_HEADS // SSM_GROUPS
SSM_STATE = 128
SSM_CONV = 4
SSM_CHUNK = 128
SSM_BC_DIM = SSM_GROUPS * SSM_STATE
SSM_CONV_DIM = SSM_D_INNER + 2 * SSM_BC_DIM
SSM_GROUP_WIDTH = SSM_D_INNER // SSM_GROUPS

DIFF_HEADS = 8
DIFF_HEAD_DIM = 64
DIFF_V_DIM = 128
DIFF_WIDTH = 1024
ATT_BLOCK = 256

REL_BUCKETS = 32
REL_MAX_DIST = 128

MEM_HEADS = 4
MEM_HEAD_DIM = 256
MEM_WIDTH = 1024

N_BRANCHES = 3
LAM_INIT = 0.8 - 0.6 * math.exp(-0.3 * 0)

OFF_XBC = 0
OFF_Z = OFF_XBC + SSM_CONV_DIM
OFF_DQ = OFF_Z + SSM_D_INNER
OFF_DK = OFF_DQ + DIFF_WIDTH
OFF_DV = OFF_DK + DIFF_WIDTH
OFF_DG = OFF_DV + DIFF_WIDTH
OFF_MQ = OFF_DG + DIFF_WIDTH
OFF_MG = OFF_MQ + MEM_WIDTH
OFF_GATE = OFF_MG + MEM_WIDTH
PROJ_DIM = OFF_GATE + N_BRANCHES * D_MODEL
DT_PAD = 128

_REF_SIZES = (SSM_D_INNER, SSM_CONV_DIM, SSM_HEADS, DIFF_WIDTH, DIFF_WIDTH, DIFF_WIDTH,
              DIFF_WIDTH, MEM_WIDTH, MEM_WIDTH, N_BRANCHES * D_MODEL)
_REF_OFFS = [0] + [int(v) for v in np.cumsum(_REF_SIZES)]

VMEM_LIMIT_BYTES = 48 * 1024 * 1024


def _cparams(semantics):
    return pltpu.CompilerParams(dimension_semantics=semantics, vmem_limit_bytes=VMEM_LIMIT_BYTES)


def _dot(a, b):
    return jnp.dot(a, b, preferred_element_type=F32)


def _dot_nt(a, b):
    return lax.dot_general(a, b, (((1,), (1,)), ((), ())), preferred_element_type=F32)


def _dot_tn(a, b):
    return lax.dot_general(a, b, (((0,), (0,)), ((), ())), preferred_element_type=F32)


def _sigmoid(v):
    return 1.0 / (1.0 + jnp.exp(-v))


def _silu(v):
    return v * _sigmoid(v)


def _norm_matmul_kernel(x_ref, g_ref, w_ref, wdt_ref, o_ref, dt_ref, h_ref):
    @pl.when(pl.program_id(1) == 0)
    def _():
        x = x_ref[...]
        ms = jnp.mean(x * x, axis=-1, keepdims=True)
        h = (x * lax.rsqrt(ms + NORM_EPS) * g_ref[...]).astype(BF16)
        h_ref[...] = h
        dt_ref[...] = _dot(h, wdt_ref[...])

    o_ref[...] = _dot(h_ref[...], w_ref[...]).astype(o_ref.dtype)


def _norm_matmul(x2d, gain, w, w_small, tm, tn):
    t, d = x2d.shape
    n = w.shape[1]
    ns = w_small.shape[1]
    return pl.pallas_call(
        _norm_matmul_kernel,
        grid=(t // tm, n // tn),
        in_specs=[
            pl.BlockSpec((tm, d), lambda i, j: (i, 0)),
            pl.BlockSpec((1, d), lambda i, j: (0, 0)),
            pl.BlockSpec((d, tn), lambda i, j: (0, j)),
            pl.BlockSpec((d, ns), lambda i, j: (0, 0)),
        ],
        out_specs=[
            pl.BlockSpec((tm, tn), lambda i, j: (i, j)),
            pl.BlockSpec((tm, ns), lambda i, j: (i, 0)),
        ],
        out_shape=[jax.ShapeDtypeStruct((t, n), BF16), jax.ShapeDtypeStruct((t, ns), F32)],
        scratch_shapes=[pltpu.VMEM((tm, d), BF16)],
        compiler_params=_cparams(("parallel", "arbitrary")),
        name="norm_matmul",
    )(x2d, gain, w, w_small)


CONV_SLAB = 512
TAIL_ROWS = 8


def _cumsum_rows(v):
    n = v.shape[0]
    row = lax.broadcasted_iota(jnp.int32, v.shape, 0)
    shift = 1
    while shift < n:
        v = v + jnp.where(row >= shift, pltpu.roll(v, shift, axis=0), 0.0)
        shift *= 2
    return v


def _split_bf16(v):
    hi = v.astype(BF16)
    lo = (v - hi.astype(F32)).astype(BF16)
    return hi, lo


def _ssd_kernel(xbc_ref, z_ref, dtraw_ref, convw_ref, convb_ref, dtb_ref, alog_ref, dskip_ref,
                ng_ref, e_ref, y_ref, ubuf_ref, xs_ref, bc_ref, state_ref):
    q = SSM_CHUNK

    @pl.when(pl.program_id(1) == 0)
    def _():
        ubuf_ref[0:TAIL_ROWS, :] = jnp.zeros((TAIL_ROWS, SSM_CONV_DIM), F32)
        state_ref[...] = jnp.zeros_like(state_ref)

    for s in range(SSM_CONV_DIM // CONV_SLAB):
        c0 = s * CONV_SLAB
        cols = slice(c0, c0 + CONV_SLAB)
        u = xbc_ref[:, cols].astype(F32)
        ubuf_ref[TAIL_ROWS:TAIL_ROWS + q, cols] = u
        w = convw_ref[:, cols]
        acc = convb_ref[:, cols] + w[SSM_CONV - 1:SSM_CONV, :] * u
        for j in range(1, SSM_CONV):
            acc = acc + (w[SSM_CONV - 1 - j:SSM_CONV - j, :]
                         * ubuf_ref[TAIL_ROWS - j:TAIL_ROWS - j + q, cols])
        ubuf_ref[0:TAIL_ROWS, cols] = u[q - TAIL_ROWS:q, :]
        act = _silu(acc)
        if c0 < SSM_D_INNER:
            xs_ref[:, cols] = act
        else:
            bc_ref[:, c0 - SSM_D_INNER:c0 - SSM_D_INNER + CONV_SLAB] = act.astype(BF16)

    pre = dtraw_ref[...] + dtb_ref[...]
    dt = jnp.maximum(pre, 0.0) + jnp.log(1.0 + jnp.exp(-jnp.abs(pre)))
    a = -jnp.exp(alog_ref[...])
    acs = _cumsum_rows(dt * a)
    acs_t = acs.T
    eacs = jnp.exp(acs)
    dec = jnp.exp(acs[q - 1:q, :] - acs)
    dt_parts = _split_bf16(dt)
    eacs_parts = _split_bf16(eacs)
    dec_parts = _split_bf16(dec)

    row = lax.broadcasted_iota(jnp.int32, (q, q), 0)
    col = lax.broadcasted_iota(jnp.int32, (q, q), 1)
    causal = row >= col
    lane_head = lax.broadcasted_iota(jnp.int32, (q, SSM_GROUP_WIDTH), 1) // SSM_HEAD_DIM

    for g in range(SSM_GROUPS):
        lanes = slice(g * SSM_GROUP_WIDTH, (g + 1) * SSM_GROUP_WIDTH)
        e_g = e_ref[:, lanes]

        def expand(parts):
            return _dot(parts[0], e_g) + _dot(parts[1], e_g)

        dt_e = expand(dt_parts)
        eacs_e = expand(eacs_parts)
        dec_e = expand(dec_parts)

        xs_g = xs_ref[:, lanes]
        xdt = xs_g * dt_e
        xdt_b = xdt.astype(BF16)
        bm_g = bc_ref[:, g * SSM_STATE:(g + 1) * SSM_STATE]
        cm_g = bc_ref[:, SSM_BC_DIM + g * SSM_STATE:SSM_BC_DIM + (g + 1) * SSM_STATE]
        cb = _dot_nt(cm_g, bm_g)

        m_parts = []
        x_parts = []
        for r in range(SSM_HEADS_PER_GROUP):
            h = g * SSM_HEADS_PER_GROUP + r
            seg = acs[:, h:h + 1] - acs_t[h:h + 1, :]
            lmat = jnp.exp(jnp.where(causal, seg, -jnp.inf))
            m_parts.append((cb * lmat).astype(BF16))
            x_parts.append(jnp.where(lane_head == r, xdt_b, jnp.zeros_like(xdt_b)))
        m_cat = jnp.concatenate(m_parts, axis=1)
        x_bd = jnp.concatenate(x_parts, axis=0)
        y = _dot(m_cat, x_bd)

        st = state_ref[g]
        y = y + _dot(cm_g, st.astype(BF16)) * eacs_e
        xw = (xdt * dec_e).astype(BF16)
        state_ref[g] = st * eacs_e[q - 1:q, :] + _dot_tn(bm_g, xw)

        y = y + xs_g * dskip_ref[:, lanes]
        y = y * _silu(z_ref[:, lanes].astype(F32))
        ms = jnp.mean(y * y, axis=-1, keepdims=True)
        y = y * lax.rsqrt(ms + NORM_EPS) * ng_ref[:, lanes]
        y_ref[:, lanes] = y.astype(y_ref.dtype)


def _ssd(proj3, dt3, conv_w, conv_b, dt_bias, a_log, d_skip_e, norm_g, expand_mat):
    b, s, _ = proj3.shape
    q = SSM_CHUNK
    full = lambda shape: pl.BlockSpec(shape, lambda bi, ci: (0,) * len(shape))
    return pl.pallas_call(
        _ssd_kernel,
        grid=(b, s // q),
        in_specs=[
            pl.BlockSpec((None, q, SSM_CONV_DIM), lambda bi, ci: (bi, ci, OFF_XBC // SSM_CONV_DIM)),
            pl.BlockSpec((None, q, SSM_D_INNER), lambda bi, ci: (bi, ci, OFF_Z // SSM_D_INNER)),
            pl.BlockSpec((None, q, DT_PAD), lambda bi, ci: (bi, ci, 0)),
            full((SSM_CONV, SSM_CONV_DIM)),
            full((1, SSM_CONV_DIM)),
            full((1, DT_PAD)),
            full((1, DT_PAD)),
            full((1, SSM_D_INNER)),
            full((1, SSM_D_INNER)),
            full((DT_PAD, SSM_D_INNER)),
        ],
        out_specs=pl.BlockSpec((None, q, SSM_D_INNER), lambda bi, ci: (bi, ci, 0)),
        out_shape=jax.ShapeDtypeStruct((b, s, SSM_D_INNER), BF16),
        scratch_shapes=[
            pltpu.VMEM((TAIL_ROWS + q, SSM_CONV_DIM), F32),
            pltpu.VMEM((q, SSM_D_INNER), F32),
            pltpu.VMEM((q, 2 * SSM_BC_DIM), BF16),
            pltpu.VMEM((SSM_GROUPS, SSM_STATE, SSM_GROUP_WIDTH), F32),
        ],
        compiler_params=_cparams(("parallel", "arbitrary")),
        name="ssd",
    )(proj3, proj3, dt3, conv_w, conv_b, dt_bias, a_log, d_skip_e, norm_g, expand_mat)


def _t5_bucket_static(rel):
    n = np.maximum(rel, 0)
    max_exact = REL_BUCKETS // 2
    nf = np.maximum(n, 1).astype(np.float64)
    large = max_exact + (np.log(nf / max_exact) / math.log(REL_MAX_DIST / max_exact)
                         * (REL_BUCKETS - max_exact)).astype(np.int64)
    large = np.minimum(large, REL_BUCKETS - 1)
    return np.where(n < max_exact, n, large)


def _near_bucket_table():
    kj = np.arange(2 * ATT_BLOCK)[:, None]
    qi = np.arange(ATT_BLOCK)[None, :]
    rel = qi + ATT_BLOCK - kj
    return np.where(rel >= 0, _t5_bucket_static(rel), -1).astype(np.int32)


def _bias_table_kernel(relb_ref, bucket_ref, o_ref):
    h = pl.program_id(0)
    bucket = bucket_ref[...]
    acc = jnp.full(bucket.shape, -jnp.inf, F32)
    for bkt in range(REL_BUCKETS):
        acc = jnp.where(bucket == bkt, relb_ref[bkt, h], acc)
    o_ref[...] = acc


def _bias_table(rel_bias):
    bucket = jnp.asarray(_near_bucket_table())
    return pl.pallas_call(
        _bias_table_kernel,
        grid=(DIFF_HEADS,),
        in_specs=[
            pl.BlockSpec(memory_space=pltpu.SMEM),
            pl.BlockSpec(bucket.shape, lambda h: (0, 0)),
        ],
        out_specs=pl.BlockSpec((None,) + bucket.shape, lambda h: (h, 0, 0)),
        out_shape=jax.ShapeDtypeStruct((DIFF_HEADS,) + bucket.shape, F32),
        compiler_params=_cparams(("arbitrary",)),
        name="bias_table",
    )(rel_bias, bucket)


def _diff_attn_tile(i, far_bias, lam, q_ref, k_ref, g_ref, tab_ref, sg_ref, o_ref, vt_ref, s_refs):
    blk = ATT_BLOCK
    qq = q_ref[...]
    lane = lax.broadcasted_iota(jnp.int32, qq.shape, 1)
    scale = DIFF_HEAD_DIM ** -0.5
    maxes = []
    for c, s_ref in enumerate(s_refs):
        keep = (lane < DIFF_HEAD_DIM) if c == 0 else (lane >= DIFF_HEAD_DIM)
        qc = jnp.where(keep, qq, jnp.zeros_like(qq)) * scale
        mx = None
        for j in range(i + 1):
            near = j >= i - 1
            st = _dot_nt(k_ref[j * blk:(j + 1) * blk, :], qc)
            if near:
                t0 = (j - (i - 1)) * blk
                st = st + tab_ref[t0:t0 + blk, :]
            s_ref[j * blk:(j + 1) * blk, :] = st
            bm = jnp.max(st, axis=0, keepdims=True)
            if not near:
                bm = bm + far_bias
            mx = bm if mx is None else jnp.maximum(mx, bm)
        maxes.append(mx)
    parts = []
    for mx, s_ref in zip(maxes, s_refs):
        m_far = mx - far_bias
        l = jnp.zeros((1, blk), F32)
        ot = jnp.zeros((DIFF_V_DIM, blk), F32)
        for j in range(i + 1):
            near = j >= i - 1
            p = jnp.exp(s_ref[j * blk:(j + 1) * blk, :] - (mx if near else m_far))
            l = l + jnp.sum(p, axis=0, keepdims=True)
            ot = ot + _dot(vt_ref[:, j * blk:(j + 1) * blk], p.astype(BF16))
        parts.append(ot * (1.0 / l))
    ot = parts[0] - lam * parts[1]
    ms = jnp.mean(ot * ot, axis=0, keepdims=True)
    ot = ot * lax.rsqrt(ms + NORM_EPS) * sg_ref[...] * (1.0 - LAM_INIT)
    o_ref[...] = (ot.T * _silu(g_ref[...].astype(F32))).astype(o_ref.dtype)


def _diff_attn_kernel(farb_ref, q_ref, k_ref, v_ref, g_ref, tab_ref, lam_ref, sg_ref, o_ref,
                      vt_ref, s1_ref, s2_ref):
    hd = pl.program_id(1)
    i = pl.program_id(2)
    n_tiles = k_ref.shape[0] // ATT_BLOCK

    @pl.when(i == 0)
    def _():
        for j in range(k_ref.shape[0] // DIFF_V_DIM):
            rows = slice(j * DIFF_V_DIM, (j + 1) * DIFF_V_DIM)
            vt_ref[:, rows] = v_ref[rows, :].astype(F32).T.astype(BF16)

    far_bias = farb_ref[hd]
    lam_p = lam_ref[...]
    lam = (jnp.exp(jnp.sum(lam_p[0:1, :] * lam_p[1:2, :], axis=-1, keepdims=True))
           - jnp.exp(jnp.sum(lam_p[2:3, :] * lam_p[3:4, :], axis=-1, keepdims=True)) + LAM_INIT)

    for t in range(n_tiles):
        pl.when(i == t)(functools.partial(
            _diff_attn_tile, t, far_bias, lam, q_ref, k_ref, g_ref, tab_ref, sg_ref, o_ref,
            vt_ref, (s1_ref, s2_ref)))


def _diff_attn(proj3, table, far_bias, lam_params, subln_g_cols):
    b, s, _ = proj3.shape
    blk = ATT_BLOCK
    hw = DIFF_V_DIM
    return pl.pallas_call(
        _diff_attn_kernel,
        grid=(b, DIFF_HEADS, s // blk),
        in_specs=[
            pl.BlockSpec(memory_space=pltpu.SMEM),
            pl.BlockSpec((None, blk, hw), lambda bi, h, i: (bi, i, OFF_DQ // hw + h)),
            pl.BlockSpec((None, s, hw), lambda bi, h, i: (bi, 0, OFF_DK // hw + h)),
            pl.BlockSpec((None, s, hw), lambda bi, h, i: (bi, 0, OFF_DV // hw + h)),
            pl.BlockSpec((None, blk, hw), lambda bi, h, i: (bi, i, OFF_DG // hw + h)),
            pl.BlockSpec((None, 2 * blk, blk), lambda bi, h, i: (h, 0, 0)),
            pl.BlockSpec((4, DIFF_HEAD_DIM), lambda bi, h, i: (0, 0)),
            pl.BlockSpec((hw, blk), lambda bi, h, i: (0, 0)),
        ],
        out_specs=pl.BlockSpec((None, blk, hw), lambda bi, h, i: (bi, i, h)),
        out_shape=jax.ShapeDtypeStruct((b, s, DIFF_WIDTH), BF16),
        scratch_shapes=[
            pltpu.VMEM((hw, s), BF16),
            pltpu.VMEM((s, blk), F32),
            pltpu.VMEM((s, blk), F32),
        ],
        compiler_params=_cparams(("parallel", "parallel", "arbitrary")),
        name="diff_attn",
    )(far_bias, proj3, proj3, proj3, proj3, table, lam_params, subln_g_cols)


def _mem_attn_kernel(q_ref, g_ref, k_ref, v_ref, o_ref):
    scale = MEM_HEAD_DIM ** -0.5
    s = _dot_nt(q_ref[...] * scale, k_ref[...])
    m = jnp.max(s, axis=-1, keepdims=True)
    p = jnp.exp(s - m)
    l = jnp.sum(p, axis=-1, keepdims=True)
    o = _dot(p.astype(BF16), v_ref[...]) / l
    o_ref[...] = (o * _silu(g_ref[...].astype(F32))).astype(o_ref.dtype)


def _mem_attn(proj3, kv3, tq):
    b, s, _ = proj3.shape
    m = kv3.shape[1]
    hw = MEM_HEAD_DIM
    return pl.pallas_call(
        _mem_attn_kernel,
        grid=(b, MEM_HEADS, s // tq),
        in_specs=[
            pl.BlockSpec((None, tq, hw), lambda bi, h, i: (bi, i, OFF_MQ // hw + h)),
            pl.BlockSpec((None, tq, hw), lambda bi, h, i: (bi, i, OFF_MG // hw + h)),
            pl.BlockSpec((None, m, hw), lambda bi, h, i: (bi, 0, h)),
            pl.BlockSpec((None, m, hw), lambda bi, h, i: (bi, 0, MEM_HEADS + h)),
        ],
        out_specs=pl.BlockSpec((None, tq, hw), lambda bi, h, i: (bi, i, h)),
        out_shape=jax.ShapeDtypeStruct((b, s, MEM_WIDTH), BF16),
        compiler_params=_cparams(("parallel", "parallel", "parallel")),
        name="mem_attn",
    )(proj3, proj3, kv3, kv3)


def _merge_kernel(x_ref, ys_ref, yd_ref, ym_ref, gate_ref, ws_ref, wd_ref, wm_ref, wo_ref,
                  fg_ref, o_ref):
    d = D_MODEL
    gate = gate_ref[...].astype(F32)
    merged = (_sigmoid(gate[:, 0:d]) * _dot(ys_ref[...], ws_ref[...])
              + _sigmoid(gate[:, d:2 * d]) * _dot(yd_ref[...], wd_ref[...])
              + _sigmoid(gate[:, 2 * d:3 * d]) * _dot(ym_ref[...], wm_ref[...]))
    xo = x_ref[...] + _dot(merged.astype(BF16), wo_ref[...])
    ms = jnp.mean(xo * xo, axis=-1, keepdims=True)
    o_ref[...] = xo * lax.rsqrt(ms + NORM_EPS) * fg_ref[...]


def _merge(x2d, ys, yd, ym, proj, w_s, w_d, w_m, w_o, final_g, tm):
    t, d = x2d.shape
    gw = N_BRANCHES * D_MODEL
    rows = lambda width, cb=0: pl.BlockSpec((tm, width), lambda i: (i, cb))
    whole = lambda shape: pl.BlockSpec(shape, lambda i: (0, 0))
    return pl.pallas_call(
        _merge_kernel,
        grid=(t // tm,),
        in_specs=[
            rows(d), rows(SSM_D_INNER), rows(DIFF_WIDTH), rows(MEM_WIDTH),
            rows(gw, OFF_GATE // gw),
            whole(w_s.shape), whole(w_d.shape), whole(w_m.shape), whole(w_o.shape),
            whole((1, d)),
        ],
        out_specs=rows(d),
        out_shape=jax.ShapeDtypeStruct((t, d), F32),
        compiler_params=_cparams(("parallel",)),
        name="merge",
    )(x2d, ys, yd, ym, proj, w_s, w_d, w_m, w_o, final_g)


def _head_expand_matrix():
    e = np.zeros((DT_PAD, SSM_D_INNER), np.float32)
    for h in range(SSM_HEADS):
        e[h, h * SSM_HEAD_DIM:(h + 1) * SSM_HEAD_DIM] = 1.0
    return e


def _pad_lanes(v, width):
    return jnp.pad(v.reshape(1, -1), ((0, 0), (0, width - v.shape[-1])))


def kernel(x, mem, norm_gain, w_in, conv_w, conv_b, dt_bias, a_log, d_skip, ssm_norm_gain,
           lambda_q1, lambda_k1, lambda_q2, lambda_k2, subln_gain, mem_norm_gain, w_mem_kv,
           w_br_ssm, w_br_diff, w_br_mem, w_out, rel_bias, final_norm_gain):
    b, s, d = x.shape
    t = b * s
    assert norm_gain.shape[0] == 1, "single-layer (DEPTH == 1) kernel"
    assert s % ATT_BLOCK == 0 and s % SSM_CHUNK == 0 and d == D_MODEL
    assert int(_t5_bucket_static(np.array([ATT_BLOCK + 1]))[0]) == REL_BUCKETS - 1

    ro = _REF_OFFS
    wi = w_in[0]
    w_main = jnp.concatenate([
        wi[:, ro[1]:ro[2]],
        wi[:, ro[0]:ro[1]],
        wi[:, ro[3]:ro[10]],
    ], axis=1).astype(BF16)
    w_dt = jnp.pad(wi[:, ro[2]:ro[3]], ((0, 0), (0, DT_PAD - SSM_HEADS))).astype(BF16)

    x2d = x.reshape(t, d)
    tm = min(1024, t)
    proj, dt_raw = _norm_matmul(x2d, norm_gain[0].reshape(1, d), w_main, w_dt, tm, 1024)
    proj3 = proj.reshape(b, s, PROJ_DIM)
    dt3 = dt_raw.reshape(b, s, DT_PAD)

    y_ssm = _ssd(
        proj3, dt3, conv_w[0], conv_b[0].reshape(1, -1),
        _pad_lanes(dt_bias[0], DT_PAD), _pad_lanes(a_log[0], DT_PAD),
        jnp.repeat(d_skip[0], SSM_HEAD_DIM).reshape(1, -1), ssm_norm_gain[0].reshape(1, -1),
        jnp.asarray(_head_expand_matrix(), BF16))

    table = _bias_table(rel_bias)
    lam_params = jnp.stack([lambda_q1[0], lambda_k1[0], lambda_q2[0], lambda_k2[0]])
    subln_cols = jnp.broadcast_to(subln_gain[0].reshape(-1, 1), (DIFF_V_DIM, ATT_BLOCK))
    y_diff = _diff_attn(proj3, table, rel_bias[REL_BUCKETS - 1], lam_params, subln_cols)

    m_len = mem.shape[1]
    kv_pad = jnp.zeros((d, DT_PAD), BF16)
    kv, _ = _norm_matmul(mem.reshape(b * m_len, d), mem_norm_gain[0].reshape(1, d),
                         w_mem_kv[0].astype(BF16), kv_pad, min(1024, b * m_len), 1024)
    y_mem = _mem_attn(proj3, kv.reshape(b, m_len, 2 * MEM_WIDTH), min(1024, s))

    out = _merge(x2d, y_ssm.reshape(t, -1), y_diff.reshape(t, -1), y_mem.reshape(t, -1), proj,
                 w_br_ssm[0].astype(BF16), w_br_diff[0].astype(BF16), w_br_mem[0].astype(BF16),
                 w_out[0].astype(BF16), final_norm_gain.reshape(1, d), min(512, t))
    return out.reshape(b, s, d)
```

```python
import functools
import math

import numpy as np
import jax
import jax.numpy as jnp
from jax import lax
from jax.experimental import pallas as pl
from jax.experimental.pallas import tpu as pltpu

F32 = jnp.float32
BF16 = jnp.bfloat16

D_MODEL = 1024
NORM_EPS = 1e-5

SSM_D_INNER = 2048
SSM_HEAD_DIM = 64
SSM_HEADS = 32
SSM_GROUPS = 8
SSM_HEADS_PER_GROUP = SSM_HEADS // SSM_GROUPS
SSM_STATE = 128
SSM_CONV = 4
SSM_CHUNK = 128
SSM_BC_DIM = SSM_GROUPS * SSM_STATE
SSM_CONV_DIM = SSM_D_INNER + 2 * SSM_BC_DIM
SSM_GROUP_WIDTH = SSM_D_INNER // SSM_GROUPS

DIFF_HEADS = 8
DIFF_HEAD_DIM = 64
DIFF_V_DIM = 128
DIFF_WIDTH = 1024
ATT_BLOCK = 256

REL_BUCKETS = 32
REL_MAX_DIST = 128

MEM_HEADS = 4
MEM_HEAD_DIM = 256
MEM_WIDTH = 1024

N_BRANCHES = 3
LAM_INIT = 0.8 - 0.6 * math.exp(-0.3 * 0)
LOG2E = math.log2(math.e)
ONES_ROWS = 16

OFF_XBC = 0
OFF_Z = OFF_XBC + SSM_CONV_DIM
OFF_DQ = OFF_Z + SSM_D_INNER
OFF_DK = OFF_DQ + DIFF_WIDTH
OFF_DV = OFF_DK + DIFF_WIDTH
OFF_DG = OFF_DV + DIFF_WIDTH
OFF_MQ = OFF_DG + DIFF_WIDTH
OFF_MG = OFF_MQ + MEM_WIDTH
OFF_GATE = OFF_MG + MEM_WIDTH
PROJ_DIM = OFF_GATE + N_BRANCHES * D_MODEL
DT_PAD = 128

_REF_SIZES = (SSM_D_INNER, SSM_CONV_DIM, SSM_HEADS, DIFF_WIDTH, DIFF_WIDTH, DIFF_WIDTH,
              DIFF_WIDTH, MEM_WIDTH, MEM_WIDTH, N_BRANCHES * D_MODEL)
_REF_OFFS = [0] + [int(v) for v in np.cumsum(_REF_SIZES)]

VMEM_LIMIT_BYTES = 48 * 1024 * 1024
VMEM_COMPILER_SCRATCH_BYTES = 6 * 1024 * 1024


def _cparams(semantics):
    return pltpu.CompilerParams(dimension_semantics=semantics, vmem_limit_bytes=VMEM_LIMIT_BYTES)


def _dot(a, b):
    return jnp.dot(a, b, preferred_element_type=F32)


def _dot_nt(a, b):
    return lax.dot_general(a, b, (((1,), (1,)), ((), ())), preferred_element_type=F32)


def _dot_tn(a, b):
    return lax.dot_general(a, b, (((0,), (0,)), ((), ())), preferred_element_type=F32)


def _sigmoid(v):
    return 1.0 / (1.0 + jnp.exp2(v * -LOG2E))


def _silu(v):
    return v * _sigmoid(v)


def _norm_matmul_kernel(tn, has_small, x_ref, g_ref, w_ref, *rest):
    if has_small:
        ws_ref, o_ref, os_ref, h_ref = rest
    else:
        o_ref, h_ref = rest
    j = pl.program_id(1)

    @pl.when(j == 0)
    def _():
        x = x_ref[...]
        ms = jnp.mean(x * x, axis=-1, keepdims=True)
        h = (x * lax.rsqrt(ms + NORM_EPS) * g_ref[...]).astype(BF16)
        h_ref[...] = h
        if has_small:
            os_ref[...] = _dot(h, ws_ref[...])

    cols = pl.ds(pl.multiple_of(j * tn, tn), tn)
    o_ref[...] = _dot(h_ref[...], w_ref[:, cols]).astype(o_ref.dtype)


def _norm_matmul(x2d, gain, w, w_small, tm, tn):
    t, d = x2d.shape
    n = w.shape[1]
    has_small = w_small is not None
    resident = lambda shape: pl.BlockSpec(shape, lambda i, j: (0, 0), pipeline_mode=pl.Buffered(1))
    in_specs = [pl.BlockSpec((tm, d), lambda i, j: (i, 0)), resident((1, d)), resident((d, n))]
    out_specs = [pl.BlockSpec((tm, tn), lambda i, j: (i, j))]
    out_shape = [jax.ShapeDtypeStruct((t, n), BF16)]
    operands = [x2d, gain, w]
    if has_small:
        ns = w_small.shape[1]
        in_specs.append(resident((d, ns)))
        out_specs.append(pl.BlockSpec((tm, ns), lambda i, j: (i, 0)))
        out_shape.append(jax.ShapeDtypeStruct((t, ns), F32))
        operands.append(w_small)
    return pl.pallas_call(
        functools.partial(_norm_matmul_kernel, tn, has_small),
        grid=(t // tm, n // tn),
        in_specs=in_specs,
        out_specs=out_specs,
        out_shape=out_shape,
        scratch_shapes=[pltpu.VMEM((tm, d), BF16)],
        compiler_params=pltpu.CompilerParams(
            dimension_semantics=("parallel", "arbitrary"),
            vmem_limit_bytes=_norm_matmul_vmem_bytes(tm, tn, d, n, has_small)),
        name="norm_matmul",
    )(*operands)


def _norm_matmul_vmem_bytes(tm, tn, d, n, has_small):
    need = d * n * 2 + 2 * tm * d * 4 + tm * d * 2 + 2 * tm * tn * 2
    if has_small:
        need += d * DT_PAD * 2 + 2 * tm * DT_PAD * 4
    return need + VMEM_COMPILER_SCRATCH_BYTES


CONV_SLAB = 512
TAIL_ROWS = 8


def _cumsum_rows(v):
    n = v.shape[0]
    row = lax.broadcasted_iota(jnp.int32, v.shape, 0)
    shift = 1
    while shift < n:
        v = v + jnp.where(row >= shift, pltpu.roll(v, shift, axis=0), 0.0)
        shift *= 2
    return v


def _split_bf16(v):
    hi = v.astype(BF16)
    lo = (v - hi.astype(F32)).astype(BF16)
    return hi, lo


def _ssd_kernel(xbc_ref, z_ref, dtraw_ref, convw_ref, convb_ref, dtb_ref, alog_ref, dskip_ref,
                ng_ref, e_ref, y_ref, ubuf_ref, xs_ref, bc_ref, state_ref):
    q = SSM_CHUNK

    @pl.when(pl.program_id(1) == 0)
    def _():
        ubuf_ref[0:TAIL_ROWS, :] = jnp.zeros((TAIL_ROWS, SSM_CONV_DIM), F32)
        state_ref[...] = jnp.zeros_like(state_ref)

    for s in range(SSM_CONV_DIM // CONV_SLAB):
        c0 = s * CONV_SLAB
        cols = slice(c0, c0 + CONV_SLAB)
        u = xbc_ref[:, cols].astype(F32)
        ubuf_ref[TAIL_ROWS:TAIL_ROWS + q, cols] = u
        w = convw_ref[:, cols]
        acc = convb_ref[:, cols] + w[SSM_CONV - 1:SSM_CONV, :] * u
        for j in range(1, SSM_CONV):
            acc = acc + (w[SSM_CONV - 1 - j:SSM_CONV - j, :]
                         * ubuf_ref[TAIL_ROWS - j:TAIL_ROWS - j + q, cols])
        ubuf_ref[0:TAIL_ROWS, cols] = u[q - TAIL_ROWS:q, :]
        act = _silu(acc)
        if c0 < SSM_D_INNER:
            xs_ref[:, cols] = act
        else:
            bc_ref[:, c0 - SSM_D_INNER:c0 - SSM_D_INNER + CONV_SLAB] = act.astype(BF16)

    pre = dtraw_ref[...] + dtb_ref[...]
    dt = jnp.maximum(pre, 0.0) + jnp.log(1.0 + jnp.exp(-jnp.abs(pre)))
    a = -jnp.exp(alog_ref[...])
    acs = _cumsum_rows(dt * a)
    acs_t = acs.T
    eacs = jnp.exp(acs)
    dec = jnp.exp(acs[q - 1:q, :] - acs)
    dt_parts = _split_bf16(dt)
    eacs_parts = _split_bf16(eacs)
    dec_parts = _split_bf16(dec)

    row = lax.broadcasted_iota(jnp.int32, (q, q), 0)
    col = lax.broadcasted_iota(jnp.int32, (q, q), 1)
    causal = row >= col
    lane_head = lax.broadcasted_iota(jnp.int32, (q, SSM_GROUP_WIDTH), 1) // SSM_HEAD_DIM

    for g in range(SSM_GROUPS):
        lanes = slice(g * SSM_GROUP_WIDTH, (g + 1) * SSM_GROUP_WIDTH)
        e_g = e_ref[:, lanes]

        def expand(parts):
            return _dot(parts[0], e_g) + _dot(parts[1], e_g)

        dt_e = expand(dt_parts)
        eacs_e = expand(eacs_parts)
        dec_e = expand(dec_parts)

        xs_g = xs_ref[:, lanes]
        xdt = xs_g * dt_e
        xdt_b = xdt.astype(BF16)
        bm_g = bc_ref[:, g * SSM_STATE:(g + 1) * SSM_STATE]
        cm_g = bc_ref[:, SSM_BC_DIM + g * SSM_STATE:SSM_BC_DIM + (g + 1) * SSM_STATE]
        cb = _dot_nt(cm_g, bm_g)

        m_parts = []
        x_parts = []
        for r in range(SSM_HEADS_PER_GROUP):
            h = g * SSM_HEADS_PER_GROUP + r
            seg = acs[:, h:h + 1] - acs_t[h:h + 1, :]
            lmat = jnp.exp(jnp.where(causal, seg, -jnp.inf))
            m_parts.append((cb * lmat).astype(BF16))
            x_parts.append(jnp.where(lane_head == r, xdt_b, jnp.zeros_like(xdt_b)))
        m_cat = jnp.concatenate(m_parts, axis=1)
        x_bd = jnp.concatenate(x_parts, axis=0)
        y = _dot(m_cat, x_bd)

        st = state_ref[g]
        y = y + _dot(cm_g, st.astype(BF16)) * eacs_e
        xw = (xdt * dec_e).astype(BF16)
        state_ref[g] = st * eacs_e[q - 1:q, :] + _dot_tn(bm_g, xw)

        y = y + xs_g * dskip_ref[:, lanes]
        y = y * _silu(z_ref[:, lanes].astype(F32))
        ms = jnp.mean(y * y, axis=-1, keepdims=True)
        y = y * lax.rsqrt(ms + NORM_EPS) * ng_ref[:, lanes]
        y_ref[:, lanes] = y.astype(y_ref.dtype)


def _ssd(proj3, dt3, conv_w, conv_b, dt_bias, a_log, d_skip_e, norm_g, expand_mat):
    b, s, _ = proj3.shape
    q = SSM_CHUNK
    full = lambda shape: pl.BlockSpec(shape, lambda bi, ci: (0,) * len(shape))
    return pl.pallas_call(
        _ssd_kernel,
        grid=(b, s // q),
        in_specs=[
            pl.BlockSpec((None, q, SSM_CONV_DIM), lambda bi, ci: (bi, ci, OFF_XBC // SSM_CONV_DIM)),
            pl.BlockSpec((None, q, SSM_D_INNER), lambda bi, ci: (bi, ci, OFF_Z // SSM_D_INNER)),
            pl.BlockSpec((None, q, DT_PAD), lambda bi, ci: (bi, ci, 0)),
            full((SSM_CONV, SSM_CONV_DIM)),
            full((1, SSM_CONV_DIM)),
            full((1, DT_PAD)),
            full((1, DT_PAD)),
            full((1, SSM_D_INNER)),
            full((1, SSM_D_INNER)),
            full((DT_PAD, SSM_D_INNER)),
        ],
        out_specs=pl.BlockSpec((None, q, SSM_D_INNER), lambda bi, ci: (bi, ci, 0)),
        out_shape=jax.ShapeDtypeStruct((b, s, SSM_D_INNER), BF16),
        scratch_shapes=[
            pltpu.VMEM((TAIL_ROWS + q, SSM_CONV_DIM), F32),
            pltpu.VMEM((q, SSM_D_INNER), F32),
            pltpu.VMEM((q, 2 * SSM_BC_DIM), BF16),
            pltpu.VMEM((SSM_GROUPS, SSM_STATE, SSM_GROUP_WIDTH), F32),
        ],
        compiler_params=_cparams(("parallel", "arbitrary")),
        name="ssd",
    )(proj3, proj3, dt3, conv_w, conv_b, dt_bias, a_log, d_skip_e, norm_g, expand_mat)


def _t5_bucket_static(rel):
    n = np.maximum(rel, 0)
    max_exact = REL_BUCKETS // 2
    nf = np.maximum(n, 1).astype(np.float64)
    large = max_exact + (np.log(nf / max_exact) / math.log(REL_MAX_DIST / max_exact)
                         * (REL_BUCKETS - max_exact)).astype(np.int64)
    large = np.minimum(large, REL_BUCKETS - 1)
    return np.where(n < max_exact, n, large)


def _near_bucket_table():
    kj = np.arange(2 * ATT_BLOCK)[:, None]
    qi = np.arange(ATT_BLOCK)[None, :]
    rel = qi + ATT_BLOCK - kj
    return np.where(rel >= 0, _t5_bucket_static(rel), -1).astype(np.int32)


def _bias_table_kernel(relb_ref, bucket_ref, o_ref):
    h = pl.program_id(0)
    bucket = bucket_ref[...]
    acc = jnp.full(bucket.shape, -jnp.inf, F32)
    for bkt in range(REL_BUCKETS):
        acc = jnp.where(bucket == bkt, relb_ref[bkt, h], acc)
    o_ref[...] = acc * LOG2E


def _bias_table(rel_bias):
    bucket = jnp.asarray(_near_bucket_table())
    return pl.pallas_call(
        _bias_table_kernel,
        grid=(DIFF_HEADS,),
        in_specs=[
            pl.BlockSpec(memory_space=pltpu.SMEM),
            pl.BlockSpec(bucket.shape, lambda h: (0, 0)),
        ],
        out_specs=pl.BlockSpec((None,) + bucket.shape, lambda h: (h, 0, 0)),
        out_shape=jax.ShapeDtypeStruct((DIFF_HEADS,) + bucket.shape, F32),
        compiler_params=_cparams(("arbitrary",)),
        name="bias_table",
    )(rel_bias, bucket)


class _AttnTile:
    def __init__(self, i, far_bias, lam, q_ref, k_ref, g_ref, tab_ref, sg_ref, o_ref, vt_ref, s_refs):
        self.i = i
        self.far_bias, self.lam = far_bias, lam
        self.k_ref, self.g_ref, self.tab_ref, self.sg_ref = k_ref, g_ref, tab_ref, sg_ref
        self.o_ref, self.vt_ref, self.s_refs = o_ref, vt_ref, s_refs
        self.rows = slice(i * ATT_BLOCK, (i + 1) * ATT_BLOCK)
        qq = q_ref[self.rows, :]
        lane = lax.broadcasted_iota(jnp.int32, qq.shape, 1)
        scale = DIFF_HEAD_DIM ** -0.5 * LOG2E
        self.qc = [jnp.where(keep, qq, jnp.zeros_like(qq)) * scale
                   for keep in (lane < DIFF_HEAD_DIM, lane >= DIFF_HEAD_DIM)]
        self.mx = [None, None]
        self.acc = [jnp.zeros((DIFF_V_DIM + ONES_ROWS, ATT_BLOCK), F32) for _ in range(2)]

    def _near(self, j):
        return j >= self.i - 1

    def _score_step(self, c, j):
        blk = ATT_BLOCK
        st = _dot_nt(self.k_ref[j * blk:(j + 1) * blk, :], self.qc[c])
        if self._near(j):
            t0 = (j - (self.i - 1)) * blk
            st = st + self.tab_ref[t0:t0 + blk, :]
        self.s_refs[c][j * blk:(j + 1) * blk, :] = st
        bm = jnp.max(st, axis=0, keepdims=True)
        if not self._near(j):
            bm = bm + self.far_bias
        self.mx[c] = bm if self.mx[c] is None else jnp.maximum(self.mx[c], bm)

    def _prob_step(self, c, j):
        blk = ATT_BLOCK
        m = self.mx[c] if self._near(j) else self.mx[c] - self.far_bias
        p = jnp.exp2(self.s_refs[c][j * blk:(j + 1) * blk, :] - m)
        self.acc[c] = self.acc[c] + _dot(self.vt_ref[:, j * blk:(j + 1) * blk], p.astype(BF16))

    def score_steps(self):
        return [functools.partial(self._score_step, c, j) for j in range(self.i + 1) for c in range(2)]

    def prob_steps(self):
        return [functools.partial(self._prob_step, c, j) for j in range(self.i + 1) for c in range(2)]

    def finish(self):
        parts = [a[0:DIFF_V_DIM, :] * (1.0 / a[DIFF_V_DIM:DIFF_V_DIM + 1, :]) for a in self.acc]
        ot = parts[0] - self.lam * parts[1]
        ms = jnp.mean(ot * ot, axis=0, keepdims=True)
        ot = ot * lax.rsqrt(ms + NORM_EPS) * self.sg_ref[...] * (1.0 - LAM_INIT)
        gate = _silu(self.g_ref[self.rows, :].astype(F32))
        self.o_ref[self.rows, :] = (ot.T * gate).astype(self.o_ref.dtype)


def _interleave(first, second):
    n1, n2 = len(first), len(second)
    i1 = i2 = 0
    while i1 < n1 or i2 < n2:
        if i2 >= n2 or (i1 < n1 and i1 * n2 <= i2 * n1):
            first[i1]()
            i1 += 1
        else:
            second[i2]()
            i2 += 1


def _diff_attn_kernel(farb_ref, q_ref, k_ref, v_ref, g_ref, tab_ref, lam_ref, sg_ref, o_ref,
                      vt_ref, *score_refs):
    hd = pl.program_id(1)
    n_tiles = k_ref.shape[0] // ATT_BLOCK

    for j in range(k_ref.shape[0] // DIFF_V_DIM):
        rows = slice(j * DIFF_V_DIM, (j + 1) * DIFF_V_DIM)
        vt_ref[0:DIFF_V_DIM, rows] = v_ref[rows, :].astype(F32).T.astype(BF16)
    vt_ref[DIFF_V_DIM:, :] = jnp.ones((ONES_ROWS, k_ref.shape[0]), BF16)

    far_bias = farb_ref[hd] * LOG2E
    lam_p = lam_ref[...]
    lam = (jnp.exp(jnp.sum(lam_p[0:1, :] * lam_p[1:2, :], axis=-1, keepdims=True))
           - jnp.exp(jnp.sum(lam_p[2:3, :] * lam_p[3:4, :], axis=-1, keepdims=True)) + LAM_INIT)

    tiles = [_AttnTile(t, far_bias, lam, q_ref, k_ref, g_ref, tab_ref, sg_ref, o_ref, vt_ref,
                       score_refs[2 * t:2 * t + 2]) for t in range(n_tiles)]
    for step in tiles[0].score_steps():
        step()
    for t in range(n_tiles):
        nxt = tiles[t + 1].score_steps() if t + 1 < n_tiles else []
        _interleave(tiles[t].prob_steps(), nxt)
        tiles[t].finish()


def _diff_attn(proj3, table, far_bias, lam_params, subln_g_cols):
    b, s, _ = proj3.shape
    blk = ATT_BLOCK
    hw = DIFF_V_DIM
    head_cols = lambda off: pl.BlockSpec((None, s, hw), lambda bi, h: (bi, 0, off // hw + h))
    score_scratch = [pltpu.VMEM(((t + 1) * blk, blk), F32)
                     for t in range(s // blk) for _ in range(2)]
    return pl.pallas_call(
        _diff_attn_kernel,
        grid=(b, DIFF_HEADS),
        in_specs=[
            pl.BlockSpec(memory_space=pltpu.SMEM),
            head_cols(OFF_DQ), head_cols(OFF_DK), head_cols(OFF_DV), head_cols(OFF_DG),
            pl.BlockSpec((None, 2 * blk, blk), lambda bi, h: (h, 0, 0)),
            pl.BlockSpec((4, DIFF_HEAD_DIM), lambda bi, h: (0, 0)),
            pl.BlockSpec((hw, blk), lambda bi, h: (0, 0)),
        ],
        out_specs=pl.BlockSpec((None, s, hw), lambda bi, h: (bi, 0, h)),
        out_shape=jax.ShapeDtypeStruct((b, s, DIFF_WIDTH), BF16),
        scratch_shapes=[pltpu.VMEM((hw + ONES_ROWS, s), BF16)] + score_scratch,
        compiler_params=_cparams(("parallel", "parallel")),
        name="diff_attn",
    )(far_bias, proj3, proj3, proj3, proj3, table, lam_params, subln_g_cols)


def _mem_attn_kernel(q_ref, g_ref, k_ref, v_ref, o_ref):
    scale = MEM_HEAD_DIM ** -0.5
    s = _dot_nt(q_ref[...] * scale, k_ref[...])
    m = jnp.max(s, axis=-1, keepdims=True)
    p = jnp.exp(s - m)
    l = jnp.sum(p, axis=-1, keepdims=True)
    o = _dot(p.astype(BF16), v_ref[...]) / l
    o_ref[...] = (o * _silu(g_ref[...].astype(F32))).astype(o_ref.dtype)


def _mem_attn(proj3, kv3, tq):
    b, s, _ = proj3.shape
    m = kv3.shape[1]
    hw = MEM_HEAD_DIM
    return pl.pallas_call(
        _mem_attn_kernel,
        grid=(b, MEM_HEADS, s // tq),
        in_specs=[
            pl.BlockSpec((None, tq, hw), lambda bi, h, i: (bi, i, OFF_MQ // hw + h)),
            pl.BlockSpec((None, tq, hw), lambda bi, h, i: (bi, i, OFF_MG // hw + h)),
            pl.BlockSpec((None, m, hw), lambda bi, h, i: (bi, 0, h)),
            pl.BlockSpec((None, m, hw), lambda bi, h, i: (bi, 0, MEM_HEADS + h)),
        ],
        out_specs=pl.BlockSpec((None, tq, hw), lambda bi, h, i: (bi, i, h)),
        out_shape=jax.ShapeDtypeStruct((b, s, MEM_WIDTH), BF16),
        compiler_params=_cparams(("parallel", "parallel", "parallel")),
        name="mem_attn",
    )(proj3, proj3, kv3, kv3)


def _merge_kernel(x_ref, ys_ref, yd_ref, ym_ref, gate_ref, ws_ref, wd_ref, wm_ref, wo_ref,
                  fg_ref, o_ref):
    d = D_MODEL
    gate = gate_ref[...].astype(F32)
    merged = (_sigmoid(gate[:, 0:d]) * _dot(ys_ref[...], ws_ref[...])
              + _sigmoid(gate[:, d:2 * d]) * _dot(yd_ref[...], wd_ref[...])
              + _sigmoid(gate[:, 2 * d:3 * d]) * _dot(ym_ref[...], wm_ref[...]))
    xo = x_ref[...] + _dot(merged.astype(BF16), wo_ref[...])
    ms = jnp.mean(xo * xo, axis=-1, keepdims=True)
    o_ref[...] = xo * lax.rsqrt(ms + NORM_EPS) * fg_ref[...]


def _merge(x2d, ys, yd, ym, proj, w_s, w_d, w_m, w_o, final_g, tm):
    t, d = x2d.shape
    gw = N_BRANCHES * D_MODEL
    rows = lambda width, cb=0: pl.BlockSpec((tm, width), lambda i: (i, cb))
    whole = lambda shape: pl.BlockSpec(shape, lambda i: (0, 0))
    return pl.pallas_call(
        _merge_kernel,
        grid=(t // tm,),
        in_specs=[
            rows(d), rows(SSM_D_INNER), rows(DIFF_WIDTH), rows(MEM_WIDTH),
            rows(gw, OFF_GATE // gw),
            whole(w_s.shape), whole(w_d.shape), whole(w_m.shape), whole(w_o.shape),
            whole((1, d)),
        ],
        out_specs=rows(d),
        out_shape=jax.ShapeDtypeStruct((t, d), F32),
        compiler_params=_cparams(("parallel",)),
        name="merge",
    )(x2d, ys, yd, ym, proj, w_s, w_d, w_m, w_o, final_g)


def _head_expand_matrix():
    e = np.zeros((DT_PAD, SSM_D_INNER), np.float32)
    for h in range(SSM_HEADS):
        e[h, h * SSM_HEAD_DIM:(h + 1) * SSM_HEAD_DIM] = 1.0
    return e


def _pad_lanes(v, width):
    return jnp.pad(v.reshape(1, -1), ((0, 0), (0, width - v.shape[-1])))


def kernel(x, mem, norm_gain, w_in, conv_w, conv_b, dt_bias, a_log, d_skip, ssm_norm_gain,
           lambda_q1, lambda_k1, lambda_q2, lambda_k2, subln_gain, mem_norm_gain, w_mem_kv,
           w_br_ssm, w_br_diff, w_br_mem, w_out, rel_bias, final_norm_gain):
    b, s, d = x.shape
    t = b * s
    assert norm_gain.shape[0] == 1, "single-layer (DEPTH == 1) kernel"
    assert s % ATT_BLOCK == 0 and s % SSM_CHUNK == 0 and d == D_MODEL
    assert int(_t5_bucket_static(np.array([ATT_BLOCK + 1]))[0]) == REL_BUCKETS - 1

    ro = _REF_OFFS
    wi = w_in[0]
    w_main = jnp.concatenate([
        wi[:, ro[1]:ro[2]],
        wi[:, ro[0]:ro[1]],
        wi[:, ro[3]:ro[10]],
    ], axis=1).astype(BF16)
    w_dt = jnp.pad(wi[:, ro[2]:ro[3]], ((0, 0), (0, DT_PAD - SSM_HEADS))).astype(BF16)

    x2d = x.reshape(t, d)
    tm = min(1024, t)
    proj, dt_raw = _norm_matmul(x2d, norm_gain[0].reshape(1, d), w_main, w_dt, tm, 1024)
    proj3 = proj.reshape(b, s, PROJ_DIM)
    dt3 = dt_raw.reshape(b, s, DT_PAD)

    y_ssm = _ssd(
        proj3, dt3, conv_w[0], conv_b[0].reshape(1, -1),
        _pad_lanes(dt_bias[0], DT_PAD), _pad_lanes(a_log[0], DT_PAD),
        jnp.repeat(d_skip[0], SSM_HEAD_DIM).reshape(1, -1), ssm_norm_gain[0].reshape(1, -1),
        jnp.asarray(_head_expand_matrix(), BF16))

    table = _bias_table(rel_bias)
    lam_params = jnp.stack([lambda_q1[0], lambda_k1[0], lambda_q2[0], lambda_k2[0]])
    subln_cols = jnp.broadcast_to(subln_gain[0].reshape(-1, 1), (DIFF_V_DIM, ATT_BLOCK))
    y_diff = _diff_attn(proj3, table, rel_bias[REL_BUCKETS - 1], lam_params, subln_cols)

    m_len = mem.shape[1]
    kv, = _norm_matmul(mem.reshape(b * m_len, d), mem_norm_gain[0].reshape(1, d),
                       w_mem_kv[0].astype(BF16), None, min(1024, b * m_len), 1024)
    y_mem = _mem_attn(proj3, kv.reshape(b, m_len, 2 * MEM_WIDTH), min(1024, s))

    out = _merge(x2d, y_ssm.reshape(t, -1), y_diff.reshape(t, -1), y_mem.reshape(t, -1), proj,
                 w_br_ssm[0].astype(BF16), w_br_diff[0].astype(BF16), w_br_mem[0].astype(BF16),
                 w_out[0].astype(BF16), final_norm_gain.reshape(1, d), min(512, t))
    return out.reshape(b, s, d)
```

```python
import functools
import math

import numpy as np
import jax
import jax.numpy as jnp
from jax import lax
from jax.experimental import pallas as pl
from jax.experimental.pallas import tpu as pltpu

F32 = jnp.float32
BF16 = jnp.bfloat16

D_MODEL = 1024
NORM_EPS = 1e-5

SSM_D_INNER = 2048
SSM_HEAD_DIM = 64
SSM_HEADS = 32
SSM_GROUPS = 8
SSM_HEADS_PER_GROUP = SSM_HEADS // SSM_GROUPS
SSM_STATE = 128
SSM_CONV = 4
SSM_CHUNK = 128
SSM_BC_DIM = SSM_GROUPS * SSM_STATE
SSM_CONV_DIM = SSM_D_INNER + 2 * SSM_BC_DIM
SSM_GROUP_WIDTH = SSM_D_INNER // SSM_GROUPS

DIFF_HEADS = 8
DIFF_HEAD_DIM = 64
DIFF_V_DIM = 128
DIFF_WIDTH = 1024
ATT_BLOCK = 256

REL_BUCKETS = 32
REL_MAX_DIST = 128

MEM_HEADS = 4
MEM_HEAD_DIM = 256
MEM_WIDTH = 1024

N_BRANCHES = 3
LAM_INIT = 0.8 - 0.6 * math.exp(-0.3 * 0)
LOG2E = math.log2(math.e)
ONES_ROWS = 16

OFF_XBC = 0
OFF_Z = OFF_XBC + SSM_CONV_DIM
OFF_DQ = OFF_Z + SSM_D_INNER
OFF_DK = OFF_DQ + DIFF_WIDTH
OFF_DV = OFF_DK + DIFF_WIDTH
OFF_DG = OFF_DV + DIFF_WIDTH
OFF_MQ = OFF_DG + DIFF_WIDTH
OFF_MG = OFF_MQ + MEM_WIDTH
OFF_GATE = OFF_MG + MEM_WIDTH
PROJ_DIM = OFF_GATE + N_BRANCHES * D_MODEL
DT_PAD = 128

_REF_SIZES = (SSM_D_INNER, SSM_CONV_DIM, SSM_HEADS, DIFF_WIDTH, DIFF_WIDTH, DIFF_WIDTH,
              DIFF_WIDTH, MEM_WIDTH, MEM_WIDTH, N_BRANCHES * D_MODEL)
_REF_OFFS = [0] + [int(v) for v in np.cumsum(_REF_SIZES)]

VMEM_LIMIT_BYTES = 48 * 1024 * 1024
VMEM_COMPILER_SCRATCH_BYTES = 2 * 1024 * 1024
IN_PROJ_ROWS, IN_PROJ_COLS = 2048, 1536


def _cparams(semantics):
    return pltpu.CompilerParams(dimension_semantics=semantics, vmem_limit_bytes=VMEM_LIMIT_BYTES)


def _dot(a, b):
    return jnp.dot(a, b, preferred_element_type=F32)


def _dot_nt(a, b):
    return lax.dot_general(a, b, (((1,), (1,)), ((), ())), preferred_element_type=F32)


def _dot_tn(a, b):
    return lax.dot_general(a, b, (((0,), (0,)), ((), ())), preferred_element_type=F32)


def _sigmoid(v):
    return 1.0 / (1.0 + jnp.exp2(v * -LOG2E))


def _silu(v):
    return v * _sigmoid(v)


def _norm_matmul_kernel(has_small, x_ref, g_ref, w_ref, *rest):
    if has_small:
        ws_ref, o_ref, os_ref, h_ref = rest
    else:
        o_ref, h_ref = rest

    @pl.when(pl.program_id(1) == 0)
    def _():
        x = x_ref[...]
        ms = jnp.mean(x * x, axis=-1, keepdims=True)
        h = (x * lax.rsqrt(ms + NORM_EPS) * g_ref[...]).astype(BF16)
        h_ref[...] = h
        if has_small:
            os_ref[...] = _dot(h, ws_ref[...])

    o_ref[...] = _dot(h_ref[...], w_ref[...]).astype(o_ref.dtype)


def _norm_matmul(x2d, gain, w, w_small, tm, tn):
    t, d = x2d.shape
    n = w.shape[1]
    has_small = w_small is not None
    resident = lambda shape: pl.BlockSpec(shape, lambda i, j: (0, 0), pipeline_mode=pl.Buffered(1))
    in_specs = [pl.BlockSpec((tm, d), lambda i, j: (i, 0)), resident((1, d)),
                pl.BlockSpec((d, tn), lambda i, j: (0, j))]
    out_specs = [pl.BlockSpec((tm, tn), lambda i, j: (i, j))]
    out_shape = [jax.ShapeDtypeStruct((t, n), BF16)]
    operands = [x2d, gain, w]
    if has_small:
        ns = w_small.shape[1]
        in_specs.append(resident((d, ns)))
        out_specs.append(pl.BlockSpec((tm, ns), lambda i, j: (i, 0)))
        out_shape.append(jax.ShapeDtypeStruct((t, ns), F32))
        operands.append(w_small)
    return pl.pallas_call(
        functools.partial(_norm_matmul_kernel, has_small),
        grid=(t // tm, n // tn),
        in_specs=in_specs,
        out_specs=out_specs,
        out_shape=out_shape,
        scratch_shapes=[pltpu.VMEM((tm, d), BF16)],
        compiler_params=pltpu.CompilerParams(
            dimension_semantics=("parallel", "arbitrary"),
            vmem_limit_bytes=_norm_matmul_vmem_bytes(tm, tn, d, has_small)),
        name="norm_matmul",
    )(*operands)


def _norm_matmul_vmem_bytes(tm, tn, d, has_small):
    need = 2 * tm * d * 4 + 2 * d * tn * 2 + 2 * tm * tn * 2 + tm * d * 2 + tm * tn * 4
    if has_small:
        need += d * DT_PAD * 2 + 2 * tm * DT_PAD * 4
    return need + VMEM_COMPILER_SCRATCH_BYTES


CONV_SLAB = 256
OUT_ROWS = 32
TAIL_ROWS = 16


def _cumsum_rows(v):
    n = v.shape[0]
    row = lax.broadcasted_iota(jnp.int32, v.shape, 0)
    shift = 1
    while shift < n:
        v = v + jnp.where(row >= shift, pltpu.roll(v, shift, axis=0), 0.0)
        shift *= 2
    return v


def _split_bf16(v):
    hi = v.astype(BF16)
    lo = (v - hi.astype(F32)).astype(BF16)
    return hi, lo


def _ssd_kernel(xbc_ref, z_ref, dtraw_ref, convw_ref, convb_ref, dtb_ref, alog_ref, dskip_ref,
                ng_ref, e_ref, shift_ref, y_ref, tail_ref, xs_ref, bc_ref, state_ref):
    q = SSM_CHUNK

    @pl.when(pl.program_id(1) == 0)
    def _():
        tail_ref[...] = jnp.zeros_like(tail_ref)
        state_ref[...] = jnp.zeros_like(state_ref)

    shift_mat = shift_ref[...]

    def conv_slab(s):
        c0 = s * CONV_SLAB
        cols = slice(c0, c0 + CONV_SLAB)
        u_b = xbc_ref[:, cols]
        hist = jnp.concatenate([tail_ref[:, cols], u_b], axis=0)
        w = convw_ref[:, cols]
        acc = convb_ref[:, cols] + w[SSM_CONV - 1:SSM_CONV, :] * u_b.astype(F32)
        for j in range(1, SSM_CONV):
            shifted = _dot(shift_mat[(j - 1) * q:j * q, :], hist)
            acc = acc + w[SSM_CONV - 1 - j:SSM_CONV - j, :] * shifted
        tail_ref[:, cols] = u_b[q - TAIL_ROWS:q, :]
        act = _silu(acc)
        if c0 < SSM_D_INNER:
            xs_ref[:, cols] = act
        else:
            bc_ref[:, c0 - SSM_D_INNER:c0 - SSM_D_INNER + CONV_SLAB] = act.astype(BF16)

    pre = dtraw_ref[...] + dtb_ref[...]
    dt = jnp.maximum(pre, 0.0) + jnp.log(1.0 + jnp.exp(-jnp.abs(pre)))
    a = -jnp.exp(alog_ref[...]) * LOG2E
    acs = _cumsum_rows(dt * a)
    acs_t = acs.T
    eacs = jnp.exp2(acs)
    dec = jnp.exp2(acs[q - 1:q, :] - acs)
    dt_parts = _split_bf16(dt)
    eacs_parts = _split_bf16(eacs)
    dec_parts = _split_bf16(dec)

    row = lax.broadcasted_iota(jnp.int32, (q, q), 0)
    col = lax.broadcasted_iota(jnp.int32, (q, q), 1)
    causal = row >= col
    lane_head = lax.broadcasted_iota(jnp.int32, (q, SSM_GROUP_WIDTH), 1) // SSM_HEAD_DIM

    def group(g):
        lanes = slice(g * SSM_GROUP_WIDTH, (g + 1) * SSM_GROUP_WIDTH)
        e_g = e_ref[:, lanes]

        def expand(parts):
            return _dot(parts[0], e_g) + _dot(parts[1], e_g)

        dt_e = expand(dt_parts)
        eacs_e = expand(eacs_parts)
        dec_e = expand(dec_parts)

        xs_g = xs_ref[:, lanes]
        xdt = xs_g * dt_e
        xdt_b = xdt.astype(BF16)
        bm_g = bc_ref[:, g * SSM_STATE:(g + 1) * SSM_STATE]
        cm_g = bc_ref[:, SSM_BC_DIM + g * SSM_STATE:SSM_BC_DIM + (g + 1) * SSM_STATE]
        cb = _dot_nt(cm_g, bm_g)

        m_parts = []
        x_parts = []
        for r in range(SSM_HEADS_PER_GROUP):
            h = g * SSM_HEADS_PER_GROUP + r
            seg = acs[:, h:h + 1] - acs_t[h:h + 1, :]
            lmat = jnp.exp2(jnp.where(causal, seg, -jnp.inf))
            m_parts.append((cb * lmat).astype(BF16))
            x_parts.append(jnp.where(lane_head == r, xdt_b, jnp.zeros_like(xdt_b)))
        m_cat = jnp.concatenate(m_parts, axis=1)
        x_bd = jnp.concatenate(x_parts, axis=0)
        y_diag = _dot(m_cat, x_bd)

        st = state_ref[g]
        y_off = _dot(cm_g, st.astype(BF16))
        xw = (xdt * dec_e).astype(BF16)
        state_ref[g] = st * eacs_e[q - 1:q, :] + _dot_tn(bm_g, xw)

        for r0 in range(0, q, OUT_ROWS):
            rr = slice(r0, r0 + OUT_ROWS)
            y = y_diag[rr, :] + y_off[rr, :] * eacs_e[rr, :] + xs_g[rr, :] * dskip_ref[:, lanes]
            y = y * _silu(z_ref[rr, lanes].astype(F32))
            ms = jnp.mean(y * y, axis=-1, keepdims=True)
            y = y * lax.rsqrt(ms + NORM_EPS) * ng_ref[:, lanes]
            y_ref[rr, lanes] = y.astype(y_ref.dtype)

    xs_slabs, bc_slabs = SSM_D_INNER // CONV_SLAB, SSM_BC_DIM // CONV_SLAB
    slab_order = []
    for k in range(max(xs_slabs, bc_slabs)):
        if k < xs_slabs:
            slab_order.append(k)
        if k < bc_slabs:
            slab_order += [xs_slabs + k, xs_slabs + bc_slabs + k]
    done = 0
    for g in range(SSM_GROUPS):
        needed = (g * SSM_GROUP_WIDTH // CONV_SLAB,
                  (SSM_D_INNER + g * SSM_STATE) // CONV_SLAB,
                  (SSM_D_INNER + SSM_BC_DIM + g * SSM_STATE) // CONV_SLAB)
        ready = 1 + max(slab_order.index(s) for s in needed)
        target = min(len(slab_order), max(ready, done) + 1)
        while done < target:
            conv_slab(slab_order[done])
            done += 1
        group(g)
    assert done == len(slab_order)


def _conv_shift_matrix():
    q = SSM_CHUNK
    m = np.zeros(((SSM_CONV - 1) * q, TAIL_ROWS + q), np.float32)
    for j in range(1, SSM_CONV):
        for t in range(q):
            m[(j - 1) * q + t, TAIL_ROWS + t - j] = 1.0
    return m


def _ssd(proj3, dt3, conv_w, conv_b, dt_bias, a_log, d_skip_e, norm_g, expand_mat):
    b, s, _ = proj3.shape
    q = SSM_CHUNK
    full = lambda shape: pl.BlockSpec(shape, lambda bi, ci: (0,) * len(shape))
    return pl.pallas_call(
        _ssd_kernel,
        grid=(b, s // q),
        in_specs=[
            pl.BlockSpec((None, q, SSM_CONV_DIM), lambda bi, ci: (bi, ci, OFF_XBC // SSM_CONV_DIM)),
            pl.BlockSpec((None, q, SSM_D_INNER), lambda bi, ci: (bi, ci, OFF_Z // SSM_D_INNER)),
            pl.BlockSpec((None, q, DT_PAD), lambda bi, ci: (bi, ci, 0)),
            full((SSM_CONV, SSM_CONV_DIM)),
            full((1, SSM_CONV_DIM)),
            full((1, DT_PAD)),
            full((1, DT_PAD)),
            full((1, SSM_D_INNER)),
            full((1, SSM_D_INNER)),
            full((DT_PAD, SSM_D_INNER)),
            full(((SSM_CONV - 1) * q, TAIL_ROWS + q)),
        ],
        out_specs=pl.BlockSpec((None, q, SSM_D_INNER), lambda bi, ci: (bi, ci, 0)),
        out_shape=jax.ShapeDtypeStruct((b, s, SSM_D_INNER), BF16),
        scratch_shapes=[
            pltpu.VMEM((TAIL_ROWS, SSM_CONV_DIM), BF16),
            pltpu.VMEM((q, SSM_D_INNER), F32),
            pltpu.VMEM((q, 2 * SSM_BC_DIM), BF16),
            pltpu.VMEM((SSM_GROUPS, SSM_STATE, SSM_GROUP_WIDTH), F32),
        ],
        compiler_params=_cparams(("parallel", "arbitrary")),
        name="ssd",
    )(proj3, proj3, dt3, conv_w, conv_b, dt_bias, a_log, d_skip_e, norm_g, expand_mat,
      jnp.asarray(_conv_shift_matrix(), BF16))


def _t5_bucket_static(rel):
    n = np.maximum(rel, 0)
    max_exact = REL_BUCKETS // 2
    nf = np.maximum(n, 1).astype(np.float64)
    large = max_exact + (np.log(nf / max_exact) / math.log(REL_MAX_DIST / max_exact)
                         * (REL_BUCKETS - max_exact)).astype(np.int64)
    large = np.minimum(large, REL_BUCKETS - 1)
    return np.where(n < max_exact, n, large)


def _near_bucket_table():
    kj = np.arange(2 * ATT_BLOCK)[:, None]
    qi = np.arange(ATT_BLOCK)[None, :]
    rel = qi + ATT_BLOCK - kj
    return np.where(rel >= 0, _t5_bucket_static(rel), -1).astype(np.int32)


def _bias_table_kernel(relb_ref, bucket_ref, o_ref):
    h = pl.program_id(0)
    bucket = bucket_ref[...]
    acc = jnp.full(bucket.shape, -jnp.inf, F32)
    for bkt in range(REL_BUCKETS):
        acc = jnp.where(bucket == bkt, relb_ref[bkt, h], acc)
    o_ref[...] = acc * LOG2E


def _bias_table(rel_bias):
    bucket = jnp.asarray(_near_bucket_table())
    return pl.pallas_call(
        _bias_table_kernel,
        grid=(DIFF_HEADS,),
        in_specs=[
            pl.BlockSpec(memory_space=pltpu.SMEM),
            pl.BlockSpec(bucket.shape, lambda h: (0, 0)),
        ],
        out_specs=pl.BlockSpec((None,) + bucket.shape, lambda h: (h, 0, 0)),
        out_shape=jax.ShapeDtypeStruct((DIFF_HEADS,) + bucket.shape, F32),
        compiler_params=_cparams(("arbitrary",)),
        name="bias_table",
    )(rel_bias, bucket)


class _AttnTile:
    def __init__(self, i, far_bias, lam, q_ref, k_ref, g_ref, tab_ref, sg_ref, o_ref, vt_ref, s_refs):
        self.i = i
        self.far_bias, self.lam = far_bias, lam
        self.k_ref, self.g_ref, self.tab_ref, self.sg_ref = k_ref, g_ref, tab_ref, sg_ref
        self.o_ref, self.vt_ref, self.s_refs = o_ref, vt_ref, s_refs
        self.rows = slice(i * ATT_BLOCK, (i + 1) * ATT_BLOCK)
        qq = q_ref[self.rows, :]
        lane = lax.broadcasted_iota(jnp.int32, qq.shape, 1)
        scale = DIFF_HEAD_DIM ** -0.5 * LOG2E
        self.qc = [jnp.where(keep, qq, jnp.zeros_like(qq)) * scale
                   for keep in (lane < DIFF_HEAD_DIM, lane >= DIFF_HEAD_DIM)]
        self.mx = [None, None]
        self.acc = [jnp.zeros((DIFF_V_DIM + ONES_ROWS, ATT_BLOCK), F32) for _ in range(2)]

    def _near(self, j):
        return j >= self.i - 1

    def _score_step(self, c, j):
        blk = ATT_BLOCK
        st = _dot_nt(self.k_ref[j * blk:(j + 1) * blk, :], self.qc[c])
        if self._near(j):
            t0 = (j - (self.i - 1)) * blk
            st = st + self.tab_ref[t0:t0 + blk, :]
        self.s_refs[c][j * blk:(j + 1) * blk, :] = st
        bm = jnp.max(st, axis=0, keepdims=True)
        if not self._near(j):
            bm = bm + self.far_bias
        self.mx[c] = bm if self.mx[c] is None else jnp.maximum(self.mx[c], bm)

    def _prob_step(self, c, j):
        blk = ATT_BLOCK
        m = self.mx[c] if self._near(j) else self.mx[c] - self.far_bias
        p = jnp.exp2(self.s_refs[c][j * blk:(j + 1) * blk, :] - m)
        self.acc[c] = self.acc[c] + _dot(self.vt_ref[:, j * blk:(j + 1) * blk], p.astype(BF16))

    def score_steps(self):
        return [functools.partial(self._score_step, c, j) for j in range(self.i + 1) for c in range(2)]

    def prob_steps(self):
        return [functools.partial(self._prob_step, c, j) for j in range(self.i + 1) for c in range(2)]

    def finish(self):
        parts = [a[0:DIFF_V_DIM, :] * (1.0 / a[DIFF_V_DIM:DIFF_V_DIM + 1, :]) for a in self.acc]
        ot = parts[0] - self.lam * parts[1]
        ms = jnp.mean(ot * ot, axis=0, keepdims=True)
        ot = ot * lax.rsqrt(ms + NORM_EPS) * self.sg_ref[...] * (1.0 - LAM_INIT)
        gate = _silu(self.g_ref[self.rows, :].astype(F32))
        self.o_ref[self.rows, :] = (ot.T * gate).astype(self.o_ref.dtype)


def _interleave(first, second):
    n1, n2 = len(first), len(second)
    i1 = i2 = 0
    while i1 < n1 or i2 < n2:
        if i2 >= n2 or (i1 < n1 and i1 * n2 <= i2 * n1):
            first[i1]()
            i1 += 1
        else:
            second[i2]()
            i2 += 1


def _diff_attn_kernel(farb_ref, q_ref, k_ref, v_ref, g_ref, tab_ref, lam_ref, sg_ref, o_ref,
                      vt_ref, *score_refs):
    hd = pl.program_id(1)
    n_tiles = k_ref.shape[0] // ATT_BLOCK

    for j in range(k_ref.shape[0] // DIFF_V_DIM):
        rows = slice(j * DIFF_V_DIM, (j + 1) * DIFF_V_DIM)
        vt_ref[0:DIFF_V_DIM, rows] = v_ref[rows, :].astype(F32).T.astype(BF16)
    vt_ref[DIFF_V_DIM:, :] = jnp.ones((ONES_ROWS, k_ref.shape[0]), BF16)

    far_bias = farb_ref[hd] * LOG2E
    lam_p = lam_ref[...]
    lam = (jnp.exp(jnp.sum(lam_p[0:1, :] * lam_p[1:2, :], axis=-1, keepdims=True))
           - jnp.exp(jnp.sum(lam_p[2:3, :] * lam_p[3:4, :], axis=-1, keepdims=True)) + LAM_INIT)

    tiles = [_AttnTile(t, far_bias, lam, q_ref, k_ref, g_ref, tab_ref, sg_ref, o_ref, vt_ref,
                       score_refs[2 * t:2 * t + 2]) for t in range(n_tiles)]
    for step in tiles[0].score_steps():
        step()
    for t in range(n_tiles):
        nxt = tiles[t + 1].score_steps() if t + 1 < n_tiles else []
        _interleave(tiles[t].prob_steps(), nxt)
        tiles[t].finish()


def _diff_attn(proj3, table, far_bias, lam_params, subln_g_cols):
    b, s, _ = proj3.shape
    blk = ATT_BLOCK
    hw = DIFF_V_DIM
    head_cols = lambda off: pl.BlockSpec((None, s, hw), lambda bi, h: (bi, 0, off // hw + h))
    score_scratch = [pltpu.VMEM(((t + 1) * blk, blk), F32)
                     for t in range(s // blk) for _ in range(2)]
    return pl.pallas_call(
        _diff_attn_kernel,
        grid=(b, DIFF_HEADS),
        in_specs=[
            pl.BlockSpec(memory_space=pltpu.SMEM),
            head_cols(OFF_DQ), head_cols(OFF_DK), head_cols(OFF_DV), head_cols(OFF_DG),
            pl.BlockSpec((None, 2 * blk, blk), lambda bi, h: (h, 0, 0)),
            pl.BlockSpec((4, DIFF_HEAD_DIM), lambda bi, h: (0, 0)),
            pl.BlockSpec((hw, blk), lambda bi, h: (0, 0)),
        ],
        out_specs=pl.BlockSpec((None, s, hw), lambda bi, h: (bi, 0, h)),
        out_shape=jax.ShapeDtypeStruct((b, s, DIFF_WIDTH), BF16),
        scratch_shapes=[pltpu.VMEM((hw + ONES_ROWS, s), BF16)] + score_scratch,
        compiler_params=_cparams(("parallel", "parallel")),
        name="diff_attn",
    )(far_bias, proj3, proj3, proj3, proj3, table, lam_params, subln_g_cols)


MEM_Q_TILE = 256


def _mem_attn_kernel(q_ref, g_ref, k_ref, v_ref, o_ref, vt_ref):
    hd = MEM_HEAD_DIM
    m_len = k_ref.shape[0]
    for r in range(m_len // 128):
        for c in range(hd // 128):
            tile = v_ref[r * 128:(r + 1) * 128, c * 128:(c + 1) * 128].astype(F32)
            vt_ref[c * 128:(c + 1) * 128, r * 128:(r + 1) * 128] = tile.T.astype(BF16)
    vt_ref[hd:, :] = jnp.ones((ONES_ROWS, m_len), BF16)
    scale = MEM_HEAD_DIM ** -0.5 * LOG2E
    kk = k_ref[...]
    for t in range(q_ref.shape[0] // MEM_Q_TILE):
        rows = slice(t * MEM_Q_TILE, (t + 1) * MEM_Q_TILE)
        st = _dot_nt(kk, q_ref[rows, :] * scale)
        p = jnp.exp2(st - jnp.max(st, axis=0, keepdims=True))
        acc = _dot(vt_ref[...], p.astype(BF16))
        ot = acc[0:hd, :] * (1.0 / acc[hd:hd + 1, :])
        o_ref[rows, :] = (ot.T * _silu(g_ref[rows, :].astype(F32))).astype(o_ref.dtype)


def _mem_attn(proj3, kv3):
    b, s, _ = proj3.shape
    m = kv3.shape[1]
    hw = MEM_HEAD_DIM
    return pl.pallas_call(
        _mem_attn_kernel,
        grid=(b, MEM_HEADS),
        in_specs=[
            pl.BlockSpec((None, s, hw), lambda bi, h: (bi, 0, OFF_MQ // hw + h)),
            pl.BlockSpec((None, s, hw), lambda bi, h: (bi, 0, OFF_MG // hw + h)),
            pl.BlockSpec((None, m, hw), lambda bi, h: (bi, 0, h)),
            pl.BlockSpec((None, m, hw), lambda bi, h: (bi, 0, MEM_HEADS + h)),
        ],
        out_specs=pl.BlockSpec((None, s, hw), lambda bi, h: (bi, 0, h)),
        out_shape=jax.ShapeDtypeStruct((b, s, MEM_WIDTH), BF16),
        scratch_shapes=[pltpu.VMEM((hw + ONES_ROWS, m), BF16)],
        compiler_params=_cparams(("parallel", "parallel")),
        name="mem_attn",
    )(proj3, proj3, kv3, kv3)


def _merge_kernel(x_ref, ys_ref, yd_ref, ym_ref, gate_ref, ws_ref, wd_ref, wm_ref, wo_ref,
                  fg_ref, o_ref):
    d = D_MODEL
    gate = gate_ref[...].astype(F32)
    merged = (_sigmoid(gate[:, 0:d]) * _dot(ys_ref[...], ws_ref[...])
              + _sigmoid(gate[:, d:2 * d]) * _dot(yd_ref[...], wd_ref[...])
              + _sigmoid(gate[:, 2 * d:3 * d]) * _dot(ym_ref[...], wm_ref[...]))
    xo = x_ref[...] + _dot(merged.astype(BF16), wo_ref[...])
    ms = jnp.mean(xo * xo, axis=-1, keepdims=True)
    o_ref[...] = xo * lax.rsqrt(ms + NORM_EPS) * fg_ref[...]


def _merge(x2d, ys, yd, ym, proj, w_s, w_d, w_m, w_o, final_g, tm):
    t, d = x2d.shape
    gw = N_BRANCHES * D_MODEL
    rows = lambda width, cb=0: pl.BlockSpec((tm, width), lambda i: (i, cb))
    whole = lambda shape: pl.BlockSpec(shape, lambda i: (0, 0))
    return pl.pallas_call(
        _merge_kernel,
        grid=(t // tm,),
        in_specs=[
            rows(d), rows(SSM_D_INNER), rows(DIFF_WIDTH), rows(MEM_WIDTH),
            rows(gw, OFF_GATE // gw),
            whole(w_s.shape), whole(w_d.shape), whole(w_m.shape), whole(w_o.shape),
            whole((1, d)),
        ],
        out_specs=rows(d),
        out_shape=jax.ShapeDtypeStruct((t, d), F32),
        compiler_params=_cparams(("parallel",)),
        name="merge",
    )(x2d, ys, yd, ym, proj, w_s, w_d, w_m, w_o, final_g)


def _head_expand_matrix():
    e = np.zeros((DT_PAD, SSM_D_INNER), np.float32)
    for h in range(SSM_HEADS):
        e[h, h * SSM_HEAD_DIM:(h + 1) * SSM_HEAD_DIM] = 1.0
    return e


def _pad_lanes(v, width):
    return jnp.pad(v.reshape(1, -1), ((0, 0), (0, width - v.shape[-1])))


def kernel(x, mem, norm_gain, w_in, conv_w, conv_b, dt_bias, a_log, d_skip, ssm_norm_gain,
           lambda_q1, lambda_k1, lambda_q2, lambda_k2, subln_gain, mem_norm_gain, w_mem_kv,
           w_br_ssm, w_br_diff, w_br_mem, w_out, rel_bias, final_norm_gain):
    b, s, d = x.shape
    t = b * s
    assert norm_gain.shape[0] == 1, "single-layer (DEPTH == 1) kernel"
    assert s % ATT_BLOCK == 0 and s % SSM_CHUNK == 0 and d == D_MODEL
    assert int(_t5_bucket_static(np.array([ATT_BLOCK + 1]))[0]) == REL_BUCKETS - 1

    ro = _REF_OFFS
    wi = w_in[0].astype(BF16)
    w_main = jnp.concatenate([
        wi[:, ro[1]:ro[2]],
        wi[:, ro[0]:ro[1]],
        wi[:, ro[3]:ro[10]],
    ], axis=1)
    w_dt = jnp.pad(wi[:, ro[2]:ro[3]], ((0, 0), (0, DT_PAD - SSM_HEADS)))

    x2d = x.reshape(t, d)
    proj, dt_raw = _norm_matmul(x2d, norm_gain[0].reshape(1, d), w_main, w_dt,
                                min(IN_PROJ_ROWS, t), IN_PROJ_COLS)
    proj3 = proj.reshape(b, s, PROJ_DIM)
    dt3 = dt_raw.reshape(b, s, DT_PAD)

    y_ssm = _ssd(
        proj3, dt3, conv_w[0], conv_b[0].reshape(1, -1),
        _pad_lanes(dt_bias[0], DT_PAD), _pad_lanes(a_log[0], DT_PAD),
        jnp.repeat(d_skip[0], SSM_HEAD_DIM).reshape(1, -1), ssm_norm_gain[0].reshape(1, -1),
        jnp.asarray(_head_expand_matrix(), BF16))

    table = _bias_table(rel_bias)
    lam_params = jnp.stack([lambda_q1[0], lambda_k1[0], lambda_q2[0], lambda_k2[0]])
    subln_cols = jnp.broadcast_to(subln_gain[0].reshape(-1, 1), (DIFF_V_DIM, ATT_BLOCK))
    y_diff = _diff_attn(proj3, table, rel_bias[REL_BUCKETS - 1], lam_params, subln_cols)

    m_len = mem.shape[1]
    kv, = _norm_matmul(mem.reshape(b * m_len, d), mem_norm_gain[0].reshape(1, d),
                       w_mem_kv[0].astype(BF16), None, min(1024, b * m_len), 1024)
    y_mem = _mem_attn(proj3, kv.reshape(b, m_len, 2 * MEM_WIDTH))

    out = _merge(x2d, y_ssm.reshape(t, -1), y_diff.reshape(t, -1), y_mem.reshape(t, -1), proj,
                 w_br_ssm[0].astype(BF16), w_br_diff[0].astype(BF16), w_br_mem[0].astype(BF16),
                 w_out[0].astype(BF16), final_norm_gain.reshape(1, d), min(512, t))
    return out.reshape(b, s, d)
```

```python
import functools
import math

import numpy as np
import jax
import jax.numpy as jnp
from jax import lax
from jax.experimental import pallas as pl
from jax.experimental.pallas import tpu as pltpu

F32 = jnp.float32
BF16 = jnp.bfloat16

D_MODEL = 1024
NORM_EPS = 1e-5

SSM_D_INNER = 2048
SSM_HEAD_DIM = 64
SSM_HEADS = 32
SSM_GROUPS = 8
SSM_HEADS_PER_GROUP = SSM_HEADS // SSM_GROUPS
SSM_STATE = 128
SSM_CONV = 4
SSM_CHUNK = 128
SSM_BC_DIM = SSM_GROUPS * SSM_STATE
SSM_CONV_DIM = SSM_D_INNER + 2 * SSM_BC_DIM
SSM_GROUP_WIDTH = SSM_D_INNER // SSM_GROUPS

DIFF_HEADS = 8
DIFF_HEAD_DIM = 64
DIFF_V_DIM = 128
DIFF_WIDTH = 1024
ATT_BLOCK = 256

REL_BUCKETS = 32
REL_MAX_DIST = 128

MEM_HEADS = 4
MEM_HEAD_DIM = 256
MEM_WIDTH = 1024

N_BRANCHES = 3
LAM_INIT = 0.8 - 0.6 * math.exp(-0.3 * 0)
LOG2E = math.log2(math.e)
ONES_ROWS = 16

OFF_XBC = 0
OFF_Z = OFF_XBC + SSM_CONV_DIM
OFF_DQ = OFF_Z + SSM_D_INNER
OFF_DK = OFF_DQ + DIFF_WIDTH
OFF_DV = OFF_DK + DIFF_WIDTH
OFF_DG = OFF_DV + DIFF_WIDTH
OFF_MQ = OFF_DG + DIFF_WIDTH
OFF_MG = OFF_MQ + MEM_WIDTH
OFF_GATE = OFF_MG + MEM_WIDTH
PROJ_DIM = OFF_GATE + N_BRANCHES * D_MODEL
DT_PAD = 128

_REF_SIZES = (SSM_D_INNER, SSM_CONV_DIM, SSM_HEADS, DIFF_WIDTH, DIFF_WIDTH, DIFF_WIDTH,
              DIFF_WIDTH, MEM_WIDTH, MEM_WIDTH, N_BRANCHES * D_MODEL)
_REF_OFFS = [0] + [int(v) for v in np.cumsum(_REF_SIZES)]

VMEM_LIMIT_BYTES = 48 * 1024 * 1024
VMEM_COMPILER_SCRATCH_BYTES = 2 * 1024 * 1024
IN_PROJ_ROWS, IN_PROJ_COLS = 2048, 1536


def _cparams(semantics):
    return pltpu.CompilerParams(dimension_semantics=semantics, vmem_limit_bytes=VMEM_LIMIT_BYTES)


def _dot(a, b):
    return jnp.dot(a, b, preferred_element_type=F32)


def _dot_nt(a, b):
    return lax.dot_general(a, b, (((1,), (1,)), ((), ())), preferred_element_type=F32)


def _dot_tn(a, b):
    return lax.dot_general(a, b, (((0,), (0,)), ((), ())), preferred_element_type=F32)


def _sigmoid(v):
    return 1.0 / (1.0 + jnp.exp2(v * -LOG2E))


def _silu(v):
    return v * _sigmoid(v)


def _norm_matmul_kernel(has_small, x_ref, g_ref, w_ref, *rest):
    if has_small:
        ws_ref, o_ref, os_ref, h_ref = rest
    else:
        o_ref, h_ref = rest

    @pl.when(pl.program_id(1) == 0)
    def _():
        x = x_ref[...]
        ms = jnp.mean(x * x, axis=-1, keepdims=True)
        h = (x * lax.rsqrt(ms + NORM_EPS) * g_ref[...]).astype(BF16)
        h_ref[...] = h
        if has_small:
            os_ref[...] = _dot(h, ws_ref[...])

    o_ref[...] = _dot(h_ref[...], w_ref[...]).astype(o_ref.dtype)


def _norm_matmul(x2d, gain, w, w_small, tm, tn):
    t, d = x2d.shape
    n = w.shape[1]
    has_small = w_small is not None
    resident = lambda shape: pl.BlockSpec(shape, lambda i, j: (0, 0), pipeline_mode=pl.Buffered(1))
    in_specs = [pl.BlockSpec((tm, d), lambda i, j: (i, 0)), resident((1, d)),
                pl.BlockSpec((d, tn), lambda i, j: (0, j))]
    out_specs = [pl.BlockSpec((tm, tn), lambda i, j: (i, j))]
    out_shape = [jax.ShapeDtypeStruct((t, n), BF16)]
    operands = [x2d, gain, w]
    if has_small:
        ns = w_small.shape[1]
        in_specs.append(resident((d, ns)))
        out_specs.append(pl.BlockSpec((tm, ns), lambda i, j: (i, 0)))
        out_shape.append(jax.ShapeDtypeStruct((t, ns), F32))
        operands.append(w_small)
    return pl.pallas_call(
        functools.partial(_norm_matmul_kernel, has_small),
        grid=(t // tm, n // tn),
        in_specs=in_specs,
        out_specs=out_specs,
        out_shape=out_shape,
        scratch_shapes=[pltpu.VMEM((tm, d), BF16)],
        compiler_params=pltpu.CompilerParams(
            dimension_semantics=("parallel", "arbitrary"),
            vmem_limit_bytes=_norm_matmul_vmem_bytes(tm, tn, d, has_small)),
        name="norm_matmul",
    )(*operands)


def _norm_matmul_vmem_bytes(tm, tn, d, has_small):
    need = 2 * tm * d * 4 + 2 * d * tn * 2 + 2 * tm * tn * 2 + tm * d * 2 + tm * tn * 4
    if has_small:
        need += d * DT_PAD * 2 + 2 * tm * DT_PAD * 4
    return need + VMEM_COMPILER_SCRATCH_BYTES


CONV_SLAB = 256
OUT_ROWS = 32
TAIL_ROWS = 16


def _cumsum_rows(v):
    n = v.shape[0]
    row = lax.broadcasted_iota(jnp.int32, v.shape, 0)
    shift = 1
    while shift < n:
        v = v + jnp.where(row >= shift, pltpu.roll(v, shift, axis=0), 0.0)
        shift *= 2
    return v


def _split_bf16(v):
    hi = v.astype(BF16)
    lo = (v - hi.astype(F32)).astype(BF16)
    return jnp.concatenate([hi, lo], axis=1)


def _ssd_kernel(xbc_ref, z_ref, dtraw_ref, convw_ref, convb_ref, dtb_ref, alog_ref, dskip_ref,
                ng_ref, e_ref, shift_ref, y_ref, tail_ref, xs_ref, bc_ref, state_ref):
    q = SSM_CHUNK

    @pl.when(pl.program_id(1) == 0)
    def _():
        tail_ref[...] = jnp.zeros_like(tail_ref)
        state_ref[...] = jnp.zeros_like(state_ref)

    shift_mat = shift_ref[...]

    def conv_slab(s):
        c0 = s * CONV_SLAB
        cols = slice(c0, c0 + CONV_SLAB)
        u_b = xbc_ref[:, cols]
        hist = jnp.concatenate([tail_ref[:, cols], u_b], axis=0)
        w = convw_ref[:, cols]
        acc = convb_ref[:, cols] + w[SSM_CONV - 1:SSM_CONV, :] * u_b.astype(F32)
        shifted = _dot(shift_mat, hist)
        for j in range(1, SSM_CONV):
            acc = acc + w[SSM_CONV - 1 - j:SSM_CONV - j, :] * shifted[(j - 1) * q:j * q, :]
        tail_ref[:, cols] = u_b[q - TAIL_ROWS:q, :]
        act = _silu(acc)
        if c0 < SSM_D_INNER:
            xs_ref[:, cols] = act
        else:
            bc_ref[:, c0 - SSM_D_INNER:c0 - SSM_D_INNER + CONV_SLAB] = act.astype(BF16)

    pre = dtraw_ref[...] + dtb_ref[...]
    dt = jnp.maximum(pre, 0.0) + jnp.log(1.0 + jnp.exp(-jnp.abs(pre)))
    a = -jnp.exp(alog_ref[...]) * LOG2E
    acs = _cumsum_rows(dt * a)
    acs_t = acs.T
    eacs = jnp.exp2(acs)
    dec = jnp.exp2(acs[q - 1:q, :] - acs)
    head_vals = jnp.concatenate([_split_bf16(dt), _split_bf16(eacs), _split_bf16(dec)], axis=0)

    row = lax.broadcasted_iota(jnp.int32, (q, q), 0)
    col = lax.broadcasted_iota(jnp.int32, (q, q), 1)
    causal = row >= col
    lane_head = lax.broadcasted_iota(jnp.int32, (q, SSM_GROUP_WIDTH), 1) // SSM_HEAD_DIM

    def group(g):
        lanes = slice(g * SSM_GROUP_WIDTH, (g + 1) * SSM_GROUP_WIDTH)
        e_g = e_ref[:, lanes]
        ex = _dot(head_vals, e_g)
        dt_e, eacs_e, dec_e = ex[0:q, :], ex[q:2 * q, :], ex[2 * q:3 * q, :]

        xs_g = xs_ref[:, lanes]
        xdt = xs_g * dt_e
        xdt_b = xdt.astype(BF16)
        bm_g = bc_ref[:, g * SSM_STATE:(g + 1) * SSM_STATE]
        cm_g = bc_ref[:, SSM_BC_DIM + g * SSM_STATE:SSM_BC_DIM + (g + 1) * SSM_STATE]
        cb = _dot_nt(cm_g, bm_g)

        m_parts = []
        x_parts = []
        for r in range(SSM_HEADS_PER_GROUP):
            h = g * SSM_HEADS_PER_GROUP + r
            seg = acs[:, h:h + 1] - acs_t[h:h + 1, :]
            lmat = jnp.exp2(jnp.where(causal, seg, -jnp.inf))
            m_parts.append((cb * lmat).astype(BF16))
            x_parts.append(jnp.where(lane_head == r, xdt_b, jnp.zeros_like(xdt_b)))
        m_cat = jnp.concatenate(m_parts, axis=1)
        x_bd = jnp.concatenate(x_parts, axis=0)
        y_diag = _dot(m_cat, x_bd)

        st = state_ref[g]
        y_off = _dot(cm_g, st.astype(BF16))
        xw = (xdt * dec_e).astype(BF16)
        state_ref[g] = st * eacs_e[q - 1:q, :] + _dot_tn(bm_g, xw)

        for r0 in range(0, q, OUT_ROWS):
            rr = slice(r0, r0 + OUT_ROWS)
            y = y_diag[rr, :] + y_off[rr, :] * eacs_e[rr, :] + xs_g[rr, :] * dskip_ref[:, lanes]
            y = y * _silu(z_ref[rr, lanes].astype(F32))
            ms = jnp.mean(y * y, axis=-1, keepdims=True)
            y = y * lax.rsqrt(ms + NORM_EPS) * ng_ref[:, lanes]
            y_ref[rr, lanes] = y.astype(y_ref.dtype)

    xs_slabs, bc_slabs = SSM_D_INNER // CONV_SLAB, SSM_BC_DIM // CONV_SLAB
    slab_order = []
    for k in range(max(xs_slabs, bc_slabs)):
        if k < xs_slabs:
            slab_order.append(k)
        if k < bc_slabs:
            slab_order += [xs_slabs + k, xs_slabs + bc_slabs + k]
    done = 0
    for g in range(SSM_GROUPS):
        needed = (g * SSM_GROUP_WIDTH // CONV_SLAB,
                  (SSM_D_INNER + g * SSM_STATE) // CONV_SLAB,
                  (SSM_D_INNER + SSM_BC_DIM + g * SSM_STATE) // CONV_SLAB)
        ready = 1 + max(slab_order.index(s) for s in needed)
        target = min(len(slab_order), max(ready, done) + 1)
        while done < target:
            conv_slab(slab_order[done])
            done += 1
        group(g)
    assert done == len(slab_order)


def _conv_shift_matrix():
    q = SSM_CHUNK
    m = np.zeros(((SSM_CONV - 1) * q, TAIL_ROWS + q), np.float32)
    for j in range(1, SSM_CONV):
        for t in range(q):
            m[(j - 1) * q + t, TAIL_ROWS + t - j] = 1.0
    return m


def _ssd(proj3, dt3, conv_w, conv_b, dt_bias, a_log, d_skip_e, norm_g, expand_mat):
    b, s, _ = proj3.shape
    q = SSM_CHUNK
    full = lambda shape: pl.BlockSpec(shape, lambda bi, ci: (0,) * len(shape))
    return pl.pallas_call(
        _ssd_kernel,
        grid=(b, s // q),
        in_specs=[
            pl.BlockSpec((None, q, SSM_CONV_DIM), lambda bi, ci: (bi, ci, OFF_XBC // SSM_CONV_DIM)),
            pl.BlockSpec((None, q, SSM_D_INNER), lambda bi, ci: (bi, ci, OFF_Z // SSM_D_INNER)),
            pl.BlockSpec((None, q, DT_PAD), lambda bi, ci: (bi, ci, 0)),
            full((SSM_CONV, SSM_CONV_DIM)),
            full((1, SSM_CONV_DIM)),
            full((1, DT_PAD)),
            full((1, DT_PAD)),
            full((1, SSM_D_INNER)),
            full((1, SSM_D_INNER)),
            full((2 * DT_PAD, SSM_D_INNER)),
            full(((SSM_CONV - 1) * q, TAIL_ROWS + q)),
        ],
        out_specs=pl.BlockSpec((None, q, SSM_D_INNER), lambda bi, ci: (bi, ci, 0)),
        out_shape=jax.ShapeDtypeStruct((b, s, SSM_D_INNER), BF16),
        scratch_shapes=[
            pltpu.VMEM((TAIL_ROWS, SSM_CONV_DIM), BF16),
            pltpu.VMEM((q, SSM_D_INNER), F32),
            pltpu.VMEM((q, 2 * SSM_BC_DIM), BF16),
            pltpu.VMEM((SSM_GROUPS, SSM_STATE, SSM_GROUP_WIDTH), F32),
        ],
        compiler_params=_cparams(("parallel", "arbitrary")),
        name="ssd",
    )(proj3, proj3, dt3, conv_w, conv_b, dt_bias, a_log, d_skip_e, norm_g, expand_mat,
      jnp.asarray(_conv_shift_matrix(), BF16))


def _t5_bucket_static(rel):
    n = np.maximum(rel, 0)
    max_exact = REL_BUCKETS // 2
    nf = np.maximum(n, 1).astype(np.float64)
    large = max_exact + (np.log(nf / max_exact) / math.log(REL_MAX_DIST / max_exact)
                         * (REL_BUCKETS - max_exact)).astype(np.int64)
    large = np.minimum(large, REL_BUCKETS - 1)
    return np.where(n < max_exact, n, large)


def _near_bucket_table():
    kj = np.arange(2 * ATT_BLOCK)[:, None]
    qi = np.arange(ATT_BLOCK)[None, :]
    rel = qi + ATT_BLOCK - kj
    return np.where(rel >= 0, _t5_bucket_static(rel), -1).astype(np.int32)


def _bias_table_kernel(relb_ref, bucket_ref, o_ref):
    h = pl.program_id(0)
    bucket = bucket_ref[...]
    acc = jnp.full(bucket.shape, -jnp.inf, F32)
    for bkt in range(REL_BUCKETS):
        acc = jnp.where(bucket == bkt, relb_ref[bkt, h], acc)
    o_ref[...] = acc * LOG2E


def _bias_table(rel_bias):
    bucket = jnp.asarray(_near_bucket_table())
    return pl.pallas_call(
        _bias_table_kernel,
        grid=(DIFF_HEADS,),
        in_specs=[
            pl.BlockSpec(memory_space=pltpu.SMEM),
            pl.BlockSpec(bucket.shape, lambda h: (0, 0)),
        ],
        out_specs=pl.BlockSpec((None,) + bucket.shape, lambda h: (h, 0, 0)),
        out_shape=jax.ShapeDtypeStruct((DIFF_HEADS,) + bucket.shape, F32),
        compiler_params=_cparams(("arbitrary",)),
        name="bias_table",
    )(rel_bias, bucket)


class _AttnTile:
    def __init__(self, i, far_bias, lam, q_ref, k_ref, g_ref, tab_ref, sg_ref, o_ref, vt_ref, s_refs):
        self.i = i
        self.far_bias, self.lam = far_bias, lam
        self.k_ref, self.g_ref, self.tab_ref, self.sg_ref = k_ref, g_ref, tab_ref, sg_ref
        self.o_ref, self.vt_ref, self.s_refs = o_ref, vt_ref, s_refs
        self.rows = slice(i * ATT_BLOCK, (i + 1) * ATT_BLOCK)
        self.q_ref = q_ref
        self.qc = None
        self.mx = [None, None]
        self.probs = [[], []]

    def _scaled_queries(self):
        if self.qc is None:
            qq = self.q_ref[self.rows, :]
            lane = lax.broadcasted_iota(jnp.int32, qq.shape, 1)
            scale = DIFF_HEAD_DIM ** -0.5 * LOG2E
            self.qc = [jnp.where(keep, qq, jnp.zeros_like(qq)) * scale
                       for keep in (lane < DIFF_HEAD_DIM, lane >= DIFF_HEAD_DIM)]
        return self.qc

    def _near(self, j):
        return j >= self.i - 1

    def _score_step(self, c, j):
        blk = ATT_BLOCK
        st = _dot_nt(self.k_ref[j * blk:(j + 1) * blk, :], self._scaled_queries()[c])
        if self._near(j):
            t0 = (j - (self.i - 1)) * blk
            st = st + self.tab_ref[t0:t0 + blk, :]
        self.s_refs[c][j * blk:(j + 1) * blk, :] = st
        bm = jnp.max(st, axis=0, keepdims=True)
        if not self._near(j):
            bm = bm + self.far_bias
        self.mx[c] = bm if self.mx[c] is None else jnp.maximum(self.mx[c], bm)

    def _prob_step(self, c, j):
        blk = ATT_BLOCK
        m = self.mx[c] if self._near(j) else self.mx[c] - self.far_bias
        p = jnp.exp2(self.s_refs[c][j * blk:(j + 1) * blk, :] - m)
        self.probs[c].append(p.astype(BF16))

    def score_steps(self):
        return [functools.partial(self._score_step, c, j) for j in range(self.i + 1) for c in range(2)]

    def prob_steps(self):
        return [functools.partial(self._prob_step, c, j) for j in range(self.i + 1) for c in range(2)]

    def finish(self):
        n_keys = (self.i + 1) * ATT_BLOCK
        accs = [_dot(self.vt_ref[:, 0:n_keys], jnp.concatenate(p, axis=0)) for p in self.probs]
        parts = [a[0:DIFF_V_DIM, :] * (1.0 / a[DIFF_V_DIM:DIFF_V_DIM + 1, :]) for a in accs]
        ot = parts[0] - self.lam * parts[1]
        ms = jnp.mean(ot * ot, axis=0, keepdims=True)
        ot = ot * lax.rsqrt(ms + NORM_EPS) * self.sg_ref[...] * (1.0 - LAM_INIT)
        gate = _silu(self.g_ref[self.rows, :].astype(F32))
        self.o_ref[self.rows, :] = (ot.T * gate).astype(self.o_ref.dtype)


def _interleave(first, second):
    n1, n2 = len(first), len(second)
    i1 = i2 = 0
    while i1 < n1 or i2 < n2:
        if i2 >= n2 or (i1 < n1 and i1 * n2 <= i2 * n1):
            first[i1]()
            i1 += 1
        else:
            second[i2]()
            i2 += 1


def _diff_attn_kernel(q_ref, k_ref, v_ref, g_ref, tab_ref, lam_ref, sg_ref, o_ref,
                      vt_ref, *score_refs):
    n_tiles = k_ref.shape[0] // ATT_BLOCK

    for j in range(k_ref.shape[0] // DIFF_V_DIM):
        rows = slice(j * DIFF_V_DIM, (j + 1) * DIFF_V_DIM)
        vt_ref[0:DIFF_V_DIM, rows] = v_ref[rows, :].astype(F32).T.astype(BF16)
    vt_ref[DIFF_V_DIM:, :] = jnp.ones((ONES_ROWS, k_ref.shape[0]), BF16)

    far_bias = tab_ref[0:1, :]
    lam_p = lam_ref[...]
    lam = (jnp.exp(jnp.sum(lam_p[0:1, :] * lam_p[1:2, :], axis=-1, keepdims=True))
           - jnp.exp(jnp.sum(lam_p[2:3, :] * lam_p[3:4, :], axis=-1, keepdims=True)) + LAM_INIT)

    tiles = [_AttnTile(t, far_bias, lam, q_ref, k_ref, g_ref, tab_ref, sg_ref, o_ref, vt_ref,
                       score_refs[2 * t:2 * t + 2]) for t in range(n_tiles)]
    for step in tiles[0].score_steps():
        step()
    for t in range(n_tiles):
        nxt = tiles[t + 1].score_steps() if t + 1 < n_tiles else []
        _interleave(tiles[t].prob_steps(), nxt)
        tiles[t].finish()


def _diff_attn(proj3, table, lam_params, subln_g_cols):
    b, s, _ = proj3.shape
    blk = ATT_BLOCK
    hw = DIFF_V_DIM
    head_cols = lambda off: pl.BlockSpec((None, s, hw), lambda bi, h: (bi, 0, off // hw + h))
    score_scratch = [pltpu.VMEM(((t + 1) * blk, blk), F32)
                     for t in range(s // blk) for _ in range(2)]
    return pl.pallas_call(
        _diff_attn_kernel,
        grid=(b, DIFF_HEADS),
        in_specs=[
            head_cols(OFF_DQ), head_cols(OFF_DK), head_cols(OFF_DV), head_cols(OFF_DG),
            pl.BlockSpec((None, 2 * blk, blk), lambda bi, h: (h, 0, 0)),
            pl.BlockSpec((4, DIFF_HEAD_DIM), lambda bi, h: (0, 0)),
            pl.BlockSpec((hw, blk), lambda bi, h: (0, 0)),
        ],
        out_specs=pl.BlockSpec((None, s, hw), lambda bi, h: (bi, 0, h)),
        out_shape=jax.ShapeDtypeStruct((b, s, DIFF_WIDTH), BF16),
        scratch_shapes=[pltpu.VMEM((hw + ONES_ROWS, s), BF16)] + score_scratch,
        compiler_params=_cparams(("parallel", "parallel")),
        name="diff_attn",
    )(proj3, proj3, proj3, proj3, table, lam_params, subln_g_cols)


MEM_Q_TILE = 256


def _mem_attn_kernel(q_ref, g_ref, k_ref, v_ref, o_ref, vt_ref):
    hd = MEM_HEAD_DIM
    m_len = k_ref.shape[0]
    for r in range(m_len // 128):
        for c in range(hd // 128):
            tile = v_ref[r * 128:(r + 1) * 128, c * 128:(c + 1) * 128].astype(F32)
            vt_ref[c * 128:(c + 1) * 128, r * 128:(r + 1) * 128] = tile.T.astype(BF16)
    vt_ref[hd:, :] = jnp.ones((ONES_ROWS, m_len), BF16)
    scale = MEM_HEAD_DIM ** -0.5 * LOG2E
    kk = k_ref[...]
    for t in range(q_ref.shape[0] // MEM_Q_TILE):
        rows = slice(t * MEM_Q_TILE, (t + 1) * MEM_Q_TILE)
        st = _dot_nt(kk, q_ref[rows, :] * scale)
        p = jnp.exp2(st - jnp.max(st, axis=0, keepdims=True))
        acc = _dot(vt_ref[...], p.astype(BF16))
        ot = acc[0:hd, :] * (1.0 / acc[hd:hd + 1, :])
        o_ref[rows, :] = (ot.T * _silu(g_ref[rows, :].astype(F32))).astype(o_ref.dtype)


def _mem_attn(proj3, kv3):
    b, s, _ = proj3.shape
    m = kv3.shape[1]
    hw = MEM_HEAD_DIM
    return pl.pallas_call(
        _mem_attn_kernel,
        grid=(b, MEM_HEADS),
        in_specs=[
            pl.BlockSpec((None, s, hw), lambda bi, h: (bi, 0, OFF_MQ // hw + h)),
            pl.BlockSpec((None, s, hw), lambda bi, h: (bi, 0, OFF_MG // hw + h)),
            pl.BlockSpec((None, m, hw), lambda bi, h: (bi, 0, h)),
            pl.BlockSpec((None, m, hw), lambda bi, h: (bi, 0, MEM_HEADS + h)),
        ],
        out_specs=pl.BlockSpec((None, s, hw), lambda bi, h: (bi, 0, h)),
        out_shape=jax.ShapeDtypeStruct((b, s, MEM_WIDTH), BF16),
        scratch_shapes=[pltpu.VMEM((hw + ONES_ROWS, m), BF16)],
        compiler_params=_cparams(("parallel", "parallel")),
        name="mem_attn",
    )(proj3, proj3, kv3, kv3)


def _merge_kernel(x_ref, ys_ref, yd_ref, ym_ref, gate_ref, ws_ref, wd_ref, wm_ref, wo_ref,
                  fg_ref, o_ref):
    d = D_MODEL
    gate = gate_ref[...].astype(F32)
    merged = (_sigmoid(gate[:, 0:d]) * _dot(ys_ref[...], ws_ref[...])
              + _sigmoid(gate[:, d:2 * d]) * _dot(yd_ref[...], wd_ref[...])
              + _sigmoid(gate[:, 2 * d:3 * d]) * _dot(ym_ref[...], wm_ref[...]))
    xo = x_ref[...] + _dot(merged.astype(BF16), wo_ref[...])
    ms = jnp.mean(xo * xo, axis=-1, keepdims=True)
    o_ref[...] = xo * lax.rsqrt(ms + NORM_EPS) * fg_ref[...]


def _merge(x2d, ys, yd, ym, proj, w_s, w_d, w_m, w_o, final_g, tm):
    t, d = x2d.shape
    gw = N_BRANCHES * D_MODEL
    rows = lambda width, cb=0: pl.BlockSpec((tm, width), lambda i: (i, cb))
    whole = lambda shape: pl.BlockSpec(shape, lambda i: (0, 0))
    return pl.pallas_call(
        _merge_kernel,
        grid=(t // tm,),
        in_specs=[
            rows(d), rows(SSM_D_INNER), rows(DIFF_WIDTH), rows(MEM_WIDTH),
            rows(gw, OFF_GATE // gw),
            whole(w_s.shape), whole(w_d.shape), whole(w_m.shape), whole(w_o.shape),
            whole((1, d)),
        ],
        out_specs=rows(d),
        out_shape=jax.ShapeDtypeStruct((t, d), F32),
        compiler_params=_cparams(("parallel",)),
        name="merge",
    )(x2d, ys, yd, ym, proj, w_s, w_d, w_m, w_o, final_g)


def _head_expand_matrix():
    e = np.zeros((2 * DT_PAD, SSM_D_INNER), np.float32)
    for h in range(SSM_HEADS):
        e[h, h * SSM_HEAD_DIM:(h + 1) * SSM_HEAD_DIM] = 1.0
        e[DT_PAD + h, h * SSM_HEAD_DIM:(h + 1) * SSM_HEAD_DIM] = 1.0
    return e


def _pad_lanes(v, width):
    return jnp.pad(v.reshape(1, -1), ((0, 0), (0, width - v.shape[-1])))


def kernel(x, mem, norm_gain, w_in, conv_w, conv_b, dt_bias, a_log, d_skip, ssm_norm_gain,
           lambda_q1, lambda_k1, lambda_q2, lambda_k2, subln_gain, mem_norm_gain, w_mem_kv,
           w_br_ssm, w_br_diff, w_br_mem, w_out, rel_bias, final_norm_gain):
    b, s, d = x.shape
    t = b * s
    assert norm_gain.shape[0] == 1, "single-layer (DEPTH == 1) kernel"
    assert s % ATT_BLOCK == 0 and s % SSM_CHUNK == 0 and d == D_MODEL
    assert int(_t5_bucket_static(np.array([ATT_BLOCK]))[0]) == REL_BUCKETS - 1

    ro = _REF_OFFS
    wi = w_in[0].astype(BF16)
    w_main = jnp.concatenate([
        wi[:, ro[1]:ro[2]],
        wi[:, ro[0]:ro[1]],
        wi[:, ro[3]:ro[10]],
    ], axis=1)
    w_dt = jnp.pad(wi[:, ro[2]:ro[3]], ((0, 0), (0, DT_PAD - SSM_HEADS)))

    x2d = x.reshape(t, d)
    proj, dt_raw = _norm_matmul(x2d, norm_gain[0].reshape(1, d), w_main, w_dt,
                                min(IN_PROJ_ROWS, t), IN_PROJ_COLS)
    proj3 = proj.reshape(b, s, PROJ_DIM)
    dt3 = dt_raw.reshape(b, s, DT_PAD)

    y_ssm = _ssd(
        proj3, dt3, conv_w[0], conv_b[0].reshape(1, -1),
        _pad_lanes(dt_bias[0], DT_PAD), _pad_lanes(a_log[0], DT_PAD),
        jnp.repeat(d_skip[0], SSM_HEAD_DIM).reshape(1, -1), ssm_norm_gain[0].reshape(1, -1),
        jnp.asarray(_head_expand_matrix(), BF16))

    table = _bias_table(rel_bias)
    lam_params = jnp.stack([lambda_q1[0], lambda_k1[0], lambda_q2[0], lambda_k2[0]])
    subln_cols = jnp.broadcast_to(subln_gain[0].reshape(-1, 1), (DIFF_V_DIM, ATT_BLOCK))
    y_diff = _diff_attn(proj3, table, lam_params, subln_cols)

    m_len = mem.shape[1]
    kv, = _norm_matmul(mem.reshape(b * m_len, d), mem_norm_gain[0].reshape(1, d),
                       w_mem_kv[0].astype(BF16), None, min(1024, b * m_len), 1024)
    y_mem = _mem_attn(proj3, kv.reshape(b, m_len, 2 * MEM_WIDTH))

    out = _merge(x2d, y_ssm.reshape(t, -1), y_diff.reshape(t, -1), y_mem.reshape(t, -1), proj,
                 w_br_ssm[0].astype(BF16), w_br_diff[0].astype(BF16), w_br_mem[0].astype(BF16),
                 w_out[0].astype(BF16), final_norm_gain.reshape(1, d), min(512, t))
    return out.reshape(b, s, d)
```

```python
import functools
import math

import numpy as np
import jax
import jax.numpy as jnp
from jax import lax
from jax.experimental import pallas as pl
from jax.experimental.pallas import tpu as pltpu

F32 = jnp.float32
BF16 = jnp.bfloat16

D_MODEL = 1024
NORM_EPS = 1e-5

SSM_D_INNER = 2048
SSM_HEAD_DIM = 64
SSM_HEADS = 32
SSM_GROUPS = 8
SSM_HEADS_PER_GROUP = SSM_HEADS // SSM_GROUPS
SSM_STATE = 128
SSM_CONV = 4
SSM_CHUNK = 128
SSM_BC_DIM = SSM_GROUPS * SSM_STATE
SSM_CONV_DIM = SSM_D_INNER + 2 * SSM_BC_DIM
SSM_GROUP_WIDTH = SSM_D_INNER // SSM_GROUPS

DIFF_HEADS = 8
DIFF_HEAD_DIM = 64
DIFF_V_DIM = 128
DIFF_WIDTH = 1024
ATT_BLOCK = 256

REL_BUCKETS = 32
REL_MAX_DIST = 128

MEM_HEADS = 4
MEM_HEAD_DIM = 256
MEM_WIDTH = 1024

N_BRANCHES = 3
LAM_INIT = 0.8 - 0.6 * math.exp(-0.3 * 0)
LOG2E = math.log2(math.e)
ONES_ROWS = 16

OFF_XBC = 0
OFF_Z = OFF_XBC + SSM_CONV_DIM
OFF_DQ = OFF_Z + SSM_D_INNER
OFF_DK = OFF_DQ + DIFF_WIDTH
OFF_DV = OFF_DK + DIFF_WIDTH
OFF_DG = OFF_DV + DIFF_WIDTH
OFF_MQ = OFF_DG + DIFF_WIDTH
OFF_MG = OFF_MQ + MEM_WIDTH
OFF_GATE = OFF_MG + MEM_WIDTH
PROJ_DIM = OFF_GATE + N_BRANCHES * D_MODEL
DT_PAD = 128

_REF_SIZES = (SSM_D_INNER, SSM_CONV_DIM, SSM_HEADS, DIFF_WIDTH, DIFF_WIDTH, DIFF_WIDTH,
              DIFF_WIDTH, MEM_WIDTH, MEM_WIDTH, N_BRANCHES * D_MODEL)
_REF_OFFS = [0] + [int(v) for v in np.cumsum(_REF_SIZES)]

VMEM_LIMIT_BYTES = 48 * 1024 * 1024
VMEM_COMPILER_SCRATCH_BYTES = 2 * 1024 * 1024
IN_PROJ_ROWS, IN_PROJ_COLS = 2048, 1536


def _cparams(semantics):
    return pltpu.CompilerParams(dimension_semantics=semantics, vmem_limit_bytes=VMEM_LIMIT_BYTES)


def _dot(a, b):
    return jnp.dot(a, b, preferred_element_type=F32)


def _dot_nt(a, b):
    return lax.dot_general(a, b, (((1,), (1,)), ((), ())), preferred_element_type=F32)


def _dot_tn(a, b):
    return lax.dot_general(a, b, (((0,), (0,)), ((), ())), preferred_element_type=F32)


def _sigmoid(v):
    return 1.0 / (1.0 + jnp.exp2(v * -LOG2E))


def _silu(v):
    return v * _sigmoid(v)


def _norm_matmul_kernel(has_small, x_ref, g_ref, w_ref, *rest):
    if has_small:
        ws_ref, o_ref, os_ref, h_ref = rest
    else:
        o_ref, h_ref = rest

    @pl.when(pl.program_id(1) == 0)
    def _():
        x = x_ref[...]
        ms = jnp.mean(x * x, axis=-1, keepdims=True)
        h = (x * lax.rsqrt(ms + NORM_EPS) * g_ref[...]).astype(BF16)
        h_ref[...] = h
        if has_small:
            os_ref[...] = _dot(h, ws_ref[...])

    o_ref[...] = _dot(h_ref[...], w_ref[...]).astype(o_ref.dtype)


def _norm_matmul(x2d, gain, w, w_small, tm, tn):
    t, d = x2d.shape
    n = w.shape[1]
    has_small = w_small is not None
    resident = lambda shape: pl.BlockSpec(shape, lambda i, j: (0, 0), pipeline_mode=pl.Buffered(1))
    in_specs = [pl.BlockSpec((tm, d), lambda i, j: (i, 0)), resident((1, d)),
                pl.BlockSpec((d, tn), lambda i, j: (0, j))]
    out_specs = [pl.BlockSpec((tm, tn), lambda i, j: (i, j))]
    out_shape = [jax.ShapeDtypeStruct((t, n), BF16)]
    operands = [x2d, gain, w]
    if has_small:
        ns = w_small.shape[1]
        in_specs.append(resident((d, ns)))
        out_specs.append(pl.BlockSpec((tm, ns), lambda i, j: (i, 0)))
        out_shape.append(jax.ShapeDtypeStruct((t, ns), F32))
        operands.append(w_small)
    return pl.pallas_call(
        functools.partial(_norm_matmul_kernel, has_small),
        grid=(t // tm, n // tn),
        in_specs=in_specs,
        out_specs=out_specs,
        out_shape=out_shape,
        scratch_shapes=[pltpu.VMEM((tm, d), BF16)],
        compiler_params=pltpu.CompilerParams(
            dimension_semantics=("parallel", "arbitrary"),
            vmem_limit_bytes=_norm_matmul_vmem_bytes(tm, tn, d, has_small)),
        name="norm_matmul",
    )(*operands)


def _norm_matmul_vmem_bytes(tm, tn, d, has_small):
    need = 2 * tm * d * 4 + 2 * d * tn * 2 + 2 * tm * tn * 2 + tm * d * 2 + tm * tn * 4
    if has_small:
        need += d * DT_PAD * 2 + 2 * tm * DT_PAD * 4
    return need + VMEM_COMPILER_SCRATCH_BYTES


CONV_SLAB = 256
OUT_ROWS = 32
TAIL_ROWS = 16


def _cumsum_rows(v):
    n = v.shape[0]
    row = lax.broadcasted_iota(jnp.int32, v.shape, 0)
    shift = 1
    while shift < n:
        v = v + jnp.where(row >= shift, pltpu.roll(v, shift, axis=0), 0.0)
        shift *= 2
    return v


def _split_bf16(v):
    hi = v.astype(BF16)
    lo = (v - hi.astype(F32)).astype(BF16)
    return jnp.concatenate([hi, lo], axis=1)


def _ssd_kernel(xbc_ref, z_ref, dtraw_ref, convw_ref, convb_ref, dtb_ref, alog_ref, dskip_ref,
                ng_ref, e_ref, shift_ref, y_ref, tail_ref, xs_ref, bc_ref, state_ref):
    q = SSM_CHUNK

    @pl.when(pl.program_id(1) == 0)
    def _():
        tail_ref[...] = jnp.zeros_like(tail_ref)
        state_ref[...] = jnp.zeros_like(state_ref)

    shift_mat = shift_ref[...]

    def conv_slab(s):
        c0 = s * CONV_SLAB
        cols = slice(c0, c0 + CONV_SLAB)
        u_b = xbc_ref[:, cols]
        hist = jnp.concatenate([tail_ref[:, cols], u_b], axis=0)
        w = convw_ref[:, cols]
        w_b = w.astype(BF16)
        weighted = [hist * w_b[SSM_CONV - 1 - j:SSM_CONV - j, :] for j in range(1, SSM_CONV)]
        pad_rows = shift_mat.shape[1] - (SSM_CONV - 1) * (TAIL_ROWS + q)
        weighted.append(jnp.zeros((pad_rows, CONV_SLAB), BF16))
        acc = (convb_ref[:, cols] + w[SSM_CONV - 1:SSM_CONV, :] * u_b.astype(F32)
               + _dot(shift_mat, jnp.concatenate(weighted, axis=0)))
        tail_ref[:, cols] = u_b[q - TAIL_ROWS:q, :]
        act = _silu(acc)
        if c0 < SSM_D_INNER:
            xs_ref[:, cols] = act
        else:
            bc_ref[:, c0 - SSM_D_INNER:c0 - SSM_D_INNER + CONV_SLAB] = act.astype(BF16)

    pre = dtraw_ref[...] + dtb_ref[...]
    dt = jnp.maximum(pre, 0.0) + jnp.log(1.0 + jnp.exp(-jnp.abs(pre)))
    a = -jnp.exp(alog_ref[...]) * LOG2E
    acs = _cumsum_rows(dt * a)
    acs_t = acs.T
    eacs = jnp.exp2(acs)
    dec = jnp.exp2(acs[q - 1:q, :] - acs)
    head_vals = jnp.concatenate([_split_bf16(dt), _split_bf16(eacs), _split_bf16(dec)], axis=0)

    row = lax.broadcasted_iota(jnp.int32, (q, q), 0)
    col = lax.broadcasted_iota(jnp.int32, (q, q), 1)
    causal = row >= col
    lane_head = lax.broadcasted_iota(jnp.int32, (q, SSM_GROUP_WIDTH), 1) // SSM_HEAD_DIM

    def group(g):
        lanes = slice(g * SSM_GROUP_WIDTH, (g + 1) * SSM_GROUP_WIDTH)
        e_g = e_ref[:, lanes]
        ex = _dot(head_vals, e_g)
        dt_e, eacs_e, dec_e = ex[0:q, :], ex[q:2 * q, :], ex[2 * q:3 * q, :]

        xs_g = xs_ref[:, lanes]
        xdt = xs_g * dt_e
        xdt_b = xdt.astype(BF16)
        bm_g = bc_ref[:, g * SSM_STATE:(g + 1) * SSM_STATE]
        cm_g = bc_ref[:, SSM_BC_DIM + g * SSM_STATE:SSM_BC_DIM + (g + 1) * SSM_STATE]
        cb = _dot_nt(cm_g, bm_g)

        m_parts = []
        x_parts = []
        for r in range(SSM_HEADS_PER_GROUP):
            h = g * SSM_HEADS_PER_GROUP + r
            seg = acs[:, h:h + 1] - acs_t[h:h + 1, :]
            lmat = jnp.exp2(jnp.where(causal, seg, -jnp.inf))
            m_parts.append((cb * lmat).astype(BF16))
            x_parts.append(jnp.where(lane_head == r, xdt_b, jnp.zeros_like(xdt_b)))
        m_cat = jnp.concatenate(m_parts, axis=1)
        x_bd = jnp.concatenate(x_parts, axis=0)
        y_diag = _dot(m_cat, x_bd)

        st = state_ref[g]
        y_off = _dot(cm_g, st.astype(BF16))
        xw = (xdt * dec_e).astype(BF16)
        state_ref[g] = st * eacs_e[q - 1:q, :] + _dot_tn(bm_g, xw)

        for r0 in range(0, q, OUT_ROWS):
            rr = slice(r0, r0 + OUT_ROWS)
            y = y_diag[rr, :] + y_off[rr, :] * eacs_e[rr, :] + xs_g[rr, :] * dskip_ref[:, lanes]
            y = y * _silu(z_ref[rr, lanes].astype(F32))
            ms = jnp.mean(y * y, axis=-1, keepdims=True)
            y = y * lax.rsqrt(ms + NORM_EPS) * ng_ref[:, lanes]
            y_ref[rr, lanes] = y.astype(y_ref.dtype)

    xs_slabs, bc_slabs = SSM_D_INNER // CONV_SLAB, SSM_BC_DIM // CONV_SLAB
    slab_order = []
    for k in range(max(xs_slabs, bc_slabs)):
        if k < xs_slabs:
            slab_order.append(k)
        if k < bc_slabs:
            slab_order += [xs_slabs + k, xs_slabs + bc_slabs + k]
    done = 0
    for g in range(SSM_GROUPS):
        needed = (g * SSM_GROUP_WIDTH // CONV_SLAB,
                  (SSM_D_INNER + g * SSM_STATE) // CONV_SLAB,
                  (SSM_D_INNER + SSM_BC_DIM + g * SSM_STATE) // CONV_SLAB)
        ready = 1 + max(slab_order.index(s) for s in needed)
        target = min(len(slab_order), max(ready, done) + 1)
        while done < target:
            conv_slab(slab_order[done])
            done += 1
        group(g)
    assert done == len(slab_order)


CONV_SHIFT_K = 512


def _conv_shift_matrix():
    q = SSM_CHUNK
    m = np.zeros((q, CONV_SHIFT_K), np.float32)
    for j in range(1, SSM_CONV):
        for t in range(q):
            m[t, (j - 1) * (TAIL_ROWS + q) + TAIL_ROWS + t - j] = 1.0
    return m


def _ssd(proj3, dt3, conv_w, conv_b, dt_bias, a_log, d_skip_e, norm_g, expand_mat):
    b, s, _ = proj3.shape
    q = SSM_CHUNK
    full = lambda shape: pl.BlockSpec(shape, lambda bi, ci: (0,) * len(shape))
    return pl.pallas_call(
        _ssd_kernel,
        grid=(b, s // q),
        in_specs=[
            pl.BlockSpec((None, q, SSM_CONV_DIM), lambda bi, ci: (bi, ci, OFF_XBC // SSM_CONV_DIM)),
            pl.BlockSpec((None, q, SSM_D_INNER), lambda bi, ci: (bi, ci, OFF_Z // SSM_D_INNER)),
            pl.BlockSpec((None, q, DT_PAD), lambda bi, ci: (bi, ci, 0)),
            full((SSM_CONV, SSM_CONV_DIM)),
            full((1, SSM_CONV_DIM)),
            full((1, DT_PAD)),
            full((1, DT_PAD)),
            full((1, SSM_D_INNER)),
            full((1, SSM_D_INNER)),
            full((2 * DT_PAD, SSM_D_INNER)),
            full((q, CONV_SHIFT_K)),
        ],
        out_specs=pl.BlockSpec((None, q, SSM_D_INNER), lambda bi, ci: (bi, ci, 0)),
        out_shape=jax.ShapeDtypeStruct((b, s, SSM_D_INNER), BF16),
        scratch_shapes=[
            pltpu.VMEM((TAIL_ROWS, SSM_CONV_DIM), BF16),
            pltpu.VMEM((q, SSM_D_INNER), F32),
            pltpu.VMEM((q, 2 * SSM_BC_DIM), BF16),
            pltpu.VMEM((SSM_GROUPS, SSM_STATE, SSM_GROUP_WIDTH), F32),
        ],
        compiler_params=_cparams(("parallel", "arbitrary")),
        name="ssd",
    )(proj3, proj3, dt3, conv_w, conv_b, dt_bias, a_log, d_skip_e, norm_g, expand_mat,
      jnp.asarray(_conv_shift_matrix(), BF16))


def _t5_bucket_static(rel):
    n = np.maximum(rel, 0)
    max_exact = REL_BUCKETS // 2
    nf = np.maximum(n, 1).astype(np.float64)
    large = max_exact + (np.log(nf / max_exact) / math.log(REL_MAX_DIST / max_exact)
                         * (REL_BUCKETS - max_exact)).astype(np.int64)
    large = np.minimum(large, REL_BUCKETS - 1)
    return np.where(n < max_exact, n, large)


def _near_bucket_table():
    kj = np.arange(2 * ATT_BLOCK)[:, None]
    qi = np.arange(ATT_BLOCK)[None, :]
    rel = qi + ATT_BLOCK - kj
    return np.where(rel >= 0, _t5_bucket_static(rel), -1).astype(np.int32)


def _bias_table_kernel(relb_ref, bucket_ref, o_ref):
    h = pl.program_id(0)
    bucket = bucket_ref[...]
    acc = jnp.full(bucket.shape, -jnp.inf, F32)
    for bkt in range(REL_BUCKETS):
        acc = jnp.where(bucket == bkt, relb_ref[bkt, h], acc)
    o_ref[...] = acc * LOG2E


def _bias_table(rel_bias):
    bucket = jnp.asarray(_near_bucket_table())
    return pl.pallas_call(
        _bias_table_kernel,
        grid=(DIFF_HEADS,),
        in_specs=[
            pl.BlockSpec(memory_space=pltpu.SMEM),
            pl.BlockSpec(bucket.shape, lambda h: (0, 0)),
        ],
        out_specs=pl.BlockSpec((None,) + bucket.shape, lambda h: (h, 0, 0)),
        out_shape=jax.ShapeDtypeStruct((DIFF_HEADS,) + bucket.shape, F32),
        compiler_params=_cparams(("arbitrary",)),
        name="bias_table",
    )(rel_bias, bucket)


class _AttnTile:
    def __init__(self, i, far_bias, lam, q_ref, k_ref, g_ref, tab_ref, sg_ref, o_ref, vt_ref, s_refs):
        self.i = i
        self.far_bias, self.lam = far_bias, lam
        self.k_ref, self.g_ref, self.tab_ref, self.sg_ref = k_ref, g_ref, tab_ref, sg_ref
        self.o_ref, self.vt_ref, self.s_refs = o_ref, vt_ref, s_refs
        self.rows = slice(i * ATT_BLOCK, (i + 1) * ATT_BLOCK)
        self.q_ref = q_ref
        self.qc = None
        self.mx = [None, None]
        self.probs = [[], []]

    def _scaled_queries(self):
        if self.qc is None:
            qq = self.q_ref[self.rows, :]
            lane = lax.broadcasted_iota(jnp.int32, qq.shape, 1)
            scale = DIFF_HEAD_DIM ** -0.5 * LOG2E
            self.qc = [jnp.where(keep, qq, jnp.zeros_like(qq)) * scale
                       for keep in (lane < DIFF_HEAD_DIM, lane >= DIFF_HEAD_DIM)]
        return self.qc

    def _near(self, j):
        return j >= self.i - 1

    def _score_step(self, c, j):
        blk = ATT_BLOCK
        st = _dot_nt(self.k_ref[j * blk:(j + 1) * blk, :], self._scaled_queries()[c])
        if self._near(j):
            t0 = (j - (self.i - 1)) * blk
            st = st + self.tab_ref[t0:t0 + blk, :]
        self.s_refs[c][j * blk:(j + 1) * blk, :] = st
        bm = jnp.max(st, axis=0, keepdims=True)
        if not self._near(j):
            bm = bm + self.far_bias
        self.mx[c] = bm if self.mx[c] is None else jnp.maximum(self.mx[c], bm)

    def _prob_step(self, c, j):
        blk = ATT_BLOCK
        m = self.mx[c] if self._near(j) else self.mx[c] - self.far_bias
        p = jnp.exp2(self.s_refs[c][j * blk:(j + 1) * blk, :] - m)
        self.probs[c].append(p.astype(BF16))

    def score_steps(self):
        return [functools.partial(self._score_step, c, j) for j in range(self.i + 1) for c in range(2)]

    def prob_steps(self):
        return [functools.partial(self._prob_step, c, j) for j in range(self.i + 1) for c in range(2)]

    def finish(self):
        n_keys = (self.i + 1) * ATT_BLOCK
        accs = [_dot(self.vt_ref[:, 0:n_keys], jnp.concatenate(p, axis=0)) for p in self.probs]
        parts = [a[0:DIFF_V_DIM, :] * (1.0 / a[DIFF_V_DIM:DIFF_V_DIM + 1, :]) for a in accs]
        ot = parts[0] - self.lam * parts[1]
        ms = jnp.mean(ot * ot, axis=0, keepdims=True)
        ot = ot * lax.rsqrt(ms + NORM_EPS) * self.sg_ref[...] * (1.0 - LAM_INIT)
        gate = _silu(self.g_ref[self.rows, :].astype(F32))
        self.o_ref[self.rows, :] = (ot.T * gate).astype(self.o_ref.dtype)


def _interleave(first, second):
    n1, n2 = len(first), len(second)
    i1 = i2 = 0
    while i1 < n1 or i2 < n2:
        if i2 >= n2 or (i1 < n1 and i1 * n2 <= i2 * n1):
            first[i1]()
            i1 += 1
        else:
            second[i2]()
            i2 += 1


def _diff_attn_kernel(q_ref, k_ref, v_ref, g_ref, tab_ref, lam_ref, sg_ref, o_ref,
                      vt_ref, *score_refs):
    n_tiles = k_ref.shape[0] // ATT_BLOCK

    for j in range(k_ref.shape[0] // DIFF_V_DIM):
        rows = slice(j * DIFF_V_DIM, (j + 1) * DIFF_V_DIM)
        vt_ref[0:DIFF_V_DIM, rows] = v_ref[rows, :].astype(F32).T.astype(BF16)
    vt_ref[DIFF_V_DIM:, :] = jnp.ones((ONES_ROWS, k_ref.shape[0]), BF16)

    far_bias = tab_ref[0:1, :]
    lam_p = lam_ref[...]
    lam = (jnp.exp(jnp.sum(lam_p[0:1, :] * lam_p[1:2, :], axis=-1, keepdims=True))
           - jnp.exp(jnp.sum(lam_p[2:3, :] * lam_p[3:4, :], axis=-1, keepdims=True)) + LAM_INIT)

    tiles = [_AttnTile(t, far_bias, lam, q_ref, k_ref, g_ref, tab_ref, sg_ref, o_ref, vt_ref,
                       score_refs[2 * t:2 * t + 2]) for t in range(n_tiles)]
    for step in tiles[0].score_steps():
        step()
    for t in range(n_tiles):
        nxt = tiles[t + 1].score_steps() if t + 1 < n_tiles else []
        _interleave(tiles[t].prob_steps(), nxt)
        tiles[t].finish()


def _diff_attn(proj3, table, lam_params, subln_g_cols):
    b, s, _ = proj3.shape
    blk = ATT_BLOCK
    hw = DIFF_V_DIM
    head_cols = lambda off: pl.BlockSpec((None, s, hw), lambda bi, h: (bi, 0, off // hw + h))
    score_scratch = [pltpu.VMEM(((t + 1) * blk, blk), F32)
                     for t in range(s // blk) for _ in range(2)]
    return pl.pallas_call(
        _diff_attn_kernel,
        grid=(b, DIFF_HEADS),
        in_specs=[
            head_cols(OFF_DQ), head_cols(OFF_DK), head_cols(OFF_DV), head_cols(OFF_DG),
            pl.BlockSpec((None, 2 * blk, blk), lambda bi, h: (h, 0, 0)),
            pl.BlockSpec((4, DIFF_HEAD_DIM), lambda bi, h: (0, 0)),
            pl.BlockSpec((hw, blk), lambda bi, h: (0, 0)),
        ],
        out_specs=pl.BlockSpec((None, s, hw), lambda bi, h: (bi, 0, h)),
        out_shape=jax.ShapeDtypeStruct((b, s, DIFF_WIDTH), BF16),
        scratch_shapes=[pltpu.VMEM((hw + ONES_ROWS, s), BF16)] + score_scratch,
        compiler_params=_cparams(("parallel", "parallel")),
        name="diff_attn",
    )(proj3, proj3, proj3, proj3, table, lam_params, subln_g_cols)


MEM_Q_TILE = 256


def _mem_attn_kernel(q_ref, g_ref, k_ref, v_ref, o_ref, vt_ref):
    hd = MEM_HEAD_DIM
    m_len = k_ref.shape[0]
    for r in range(m_len // 128):
        for c in range(hd // 128):
            tile = v_ref[r * 128:(r + 1) * 128, c * 128:(c + 1) * 128].astype(F32)
            vt_ref[c * 128:(c + 1) * 128, r * 128:(r + 1) * 128] = tile.T.astype(BF16)
    vt_ref[hd:, :] = jnp.ones((ONES_ROWS, m_len), BF16)
    scale = MEM_HEAD_DIM ** -0.5 * LOG2E
    kk = k_ref[...]
    for t in range(q_ref.shape[0] // MEM_Q_TILE):
        rows = slice(t * MEM_Q_TILE, (t + 1) * MEM_Q_TILE)
        st = _dot_nt(kk, q_ref[rows, :] * scale)
        p = jnp.exp2(st - jnp.max(st, axis=0, keepdims=True))
        acc = _dot(vt_ref[...], p.astype(BF16))
        ot = acc[0:hd, :] * (1.0 / acc[hd:hd + 1, :])
        o_ref[rows, :] = (ot.T * _silu(g_ref[rows, :].astype(F32))).astype(o_ref.dtype)


def _mem_attn(proj3, kv3):
    b, s, _ = proj3.shape
    m = kv3.shape[1]
    hw = MEM_HEAD_DIM
    return pl.pallas_call(
        _mem_attn_kernel,
        grid=(b, MEM_HEADS),
        in_specs=[
            pl.BlockSpec((None, s, hw), lambda bi, h: (bi, 0, OFF_MQ // hw + h)),
            pl.BlockSpec((None, s, hw), lambda bi, h: (bi, 0, OFF_MG // hw + h)),
            pl.BlockSpec((None, m, hw), lambda bi, h: (bi, 0, h)),
            pl.BlockSpec((None, m, hw), lambda bi, h: (bi, 0, MEM_HEADS + h)),
        ],
        out_specs=pl.BlockSpec((None, s, hw), lambda bi, h: (bi, 0, h)),
        out_shape=jax.ShapeDtypeStruct((b, s, MEM_WIDTH), BF16),
        scratch_shapes=[pltpu.VMEM((hw + ONES_ROWS, m), BF16)],
        compiler_params=_cparams(("parallel", "parallel")),
        name="mem_attn",
    )(proj3, proj3, kv3, kv3)


def _merge_kernel(x_ref, ys_ref, yd_ref, ym_ref, gate_ref, ws_ref, wd_ref, wm_ref, wo_ref,
                  fg_ref, o_ref):
    d = D_MODEL
    gate = gate_ref[...].astype(F32)
    merged = (_sigmoid(gate[:, 0:d]) * _dot(ys_ref[...], ws_ref[...])
              + _sigmoid(gate[:, d:2 * d]) * _dot(yd_ref[...], wd_ref[...])
              + _sigmoid(gate[:, 2 * d:3 * d]) * _dot(ym_ref[...], wm_ref[...]))
    xo = x_ref[...] + _dot(merged.astype(BF16), wo_ref[...])
    ms = jnp.mean(xo * xo, axis=-1, keepdims=True)
    o_ref[...] = xo * lax.rsqrt(ms + NORM_EPS) * fg_ref[...]


def _merge(x2d, ys, yd, ym, proj, w_s, w_d, w_m, w_o, final_g, tm):
    t, d = x2d.shape
    gw = N_BRANCHES * D_MODEL
    rows = lambda width, cb=0: pl.BlockSpec((tm, width), lambda i: (i, cb))
    whole = lambda shape: pl.BlockSpec(shape, lambda i: (0, 0))
    return pl.pallas_call(
        _merge_kernel,
        grid=(t // tm,),
        in_specs=[
            rows(d), rows(SSM_D_INNER), rows(DIFF_WIDTH), rows(MEM_WIDTH),
            rows(gw, OFF_GATE // gw),
            whole(w_s.shape), whole(w_d.shape), whole(w_m.shape), whole(w_o.shape),
            whole((1, d)),
        ],
        out_specs=rows(d),
        out_shape=jax.ShapeDtypeStruct((t, d), F32),
        compiler_params=_cparams(("parallel",)),
        name="merge",
    )(x2d, ys, yd, ym, proj, w_s, w_d, w_m, w_o, final_g)


def _head_expand_matrix():
    e = np.zeros((2 * DT_PAD, SSM_D_INNER), np.float32)
    for h in range(SSM_HEADS):
        e[h, h * SSM_HEAD_DIM:(h + 1) * SSM_HEAD_DIM] = 1.0
        e[DT_PAD + h, h * SSM_HEAD_DIM:(h + 1) * SSM_HEAD_DIM] = 1.0
    return e


def _pad_lanes(v, width):
    return jnp.pad(v.reshape(1, -1), ((0, 0), (0, width - v.shape[-1])))


def kernel(x, mem, norm_gain, w_in, conv_w, conv_b, dt_bias, a_log, d_skip, ssm_norm_gain,
           lambda_q1, lambda_k1, lambda_q2, lambda_k2, subln_gain, mem_norm_gain, w_mem_kv,
           w_br_ssm, w_br_diff, w_br_mem, w_out, rel_bias, final_norm_gain):
    b, s, d = x.shape
    t = b * s
    assert norm_gain.shape[0] == 1, "single-layer (DEPTH == 1) kernel"
    assert s % ATT_BLOCK == 0 and s % SSM_CHUNK == 0 and d == D_MODEL
    assert int(_t5_bucket_static(np.array([ATT_BLOCK]))[0]) == REL_BUCKETS - 1

    ro = _REF_OFFS
    wi = w_in[0].astype(BF16)
    w_main = jnp.concatenate([
        wi[:, ro[1]:ro[2]],
        wi[:, ro[0]:ro[1]],
        wi[:, ro[3]:ro[10]],
    ], axis=1)
    w_dt = jnp.pad(wi[:, ro[2]:ro[3]], ((0, 0), (0, DT_PAD - SSM_HEADS)))

    x2d = x.reshape(t, d)
    proj, dt_raw = _norm_matmul(x2d, norm_gain[0].reshape(1, d), w_main, w_dt,
                                min(IN_PROJ_ROWS, t), IN_PROJ_COLS)
    proj3 = proj.reshape(b, s, PROJ_DIM)
    dt3 = dt_raw.reshape(b, s, DT_PAD)

    y_ssm = _ssd(
        proj3, dt3, conv_w[0], conv_b[0].reshape(1, -1),
        _pad_lanes(dt_bias[0], DT_PAD), _pad_lanes(a_log[0], DT_PAD),
        jnp.repeat(d_skip[0], SSM_HEAD_DIM).reshape(1, -1), ssm_norm_gain[0].reshape(1, -1),
        jnp.asarray(_head_expand_matrix(), BF16))

    table = _bias_table(rel_bias)
    lam_params = jnp.stack([lambda_q1[0], lambda_k1[0], lambda_q2[0], lambda_k2[0]])
    subln_cols = jnp.broadcast_to(subln_gain[0].reshape(-1, 1), (DIFF_V_DIM, ATT_BLOCK))
    y_diff = _diff_attn(proj3, table, lam_params, subln_cols)

    m_len = mem.shape[1]
    kv, = _norm_matmul(mem.reshape(b * m_len, d), mem_norm_gain[0].reshape(1, d),
                       w_mem_kv[0].astype(BF16), None, min(1024, b * m_len), 1024)
    y_mem = _mem_attn(proj3, kv.reshape(b, m_len, 2 * MEM_WIDTH))

    out = _merge(x2d, y_ssm.reshape(t, -1), y_diff.reshape(t, -1), y_mem.reshape(t, -1), proj,
                 w_br_ssm[0].astype(BF16), w_br_diff[0].astype(BF16), w_br_mem[0].astype(BF16),
                 w_out[0].astype(BF16), final_norm_gain.reshape(1, d), min(512, t))
    return out.reshape(b, s, d)
```

```python
import functools
import math

import numpy as np
import jax
import jax.numpy as jnp
from jax import lax
from jax.experimental import pallas as pl
from jax.experimental.pallas import tpu as pltpu

F32 = jnp.float32
BF16 = jnp.bfloat16

D_MODEL = 1024
NORM_EPS = 1e-5

SSM_D_INNER = 2048
SSM_HEAD_DIM = 64
SSM_HEADS = 32
SSM_GROUPS = 8
SSM_HEADS_PER_GROUP = SSM_HEADS // SSM_GROUPS
SSM_STATE = 128
SSM_CONV = 4
SSM_CHUNK = 128
SSM_BC_DIM = SSM_GROUPS * SSM_STATE
SSM_CONV_DIM = SSM_D_INNER + 2 * SSM_BC_DIM
SSM_GROUP_WIDTH = SSM_D_INNER // SSM_GROUPS

DIFF_HEADS = 8
DIFF_HEAD_DIM = 64
DIFF_V_DIM = 128
DIFF_WIDTH = 1024
ATT_BLOCK = 256

REL_BUCKETS = 32
REL_MAX_DIST = 128

MEM_HEADS = 4
MEM_HEAD_DIM = 256
MEM_WIDTH = 1024

N_BRANCHES = 3
LAM_INIT = 0.8 - 0.6 * math.exp(-0.3 * 0)
LOG2E = math.log2(math.e)
ONES_ROWS = 16

OFF_XBC = 0
OFF_Z = OFF_XBC + SSM_CONV_DIM
OFF_DQ = OFF_Z + SSM_D_INNER
OFF_DK = OFF_DQ + DIFF_WIDTH
OFF_DV = OFF_DK + DIFF_WIDTH
OFF_DG = OFF_DV + DIFF_WIDTH
OFF_MQ = OFF_DG + DIFF_WIDTH
OFF_MG = OFF_MQ + MEM_WIDTH
OFF_GATE = OFF_MG + MEM_WIDTH
PROJ_DIM = OFF_GATE + N_BRANCHES * D_MODEL
DT_PAD = 128

_REF_SIZES = (SSM_D_INNER, SSM_CONV_DIM, SSM_HEADS, DIFF_WIDTH, DIFF_WIDTH, DIFF_WIDTH,
              DIFF_WIDTH, MEM_WIDTH, MEM_WIDTH, N_BRANCHES * D_MODEL)
_REF_OFFS = [0] + [int(v) for v in np.cumsum(_REF_SIZES)]

VMEM_LIMIT_BYTES = 48 * 1024 * 1024
VMEM_COMPILER_SCRATCH_BYTES = 2 * 1024 * 1024
IN_PROJ_ROWS, IN_PROJ_COLS = 2048, 1536


def _cparams(semantics):
    return pltpu.CompilerParams(dimension_semantics=semantics, vmem_limit_bytes=VMEM_LIMIT_BYTES)


def _dot(a, b):
    return jnp.dot(a, b, preferred_element_type=F32)


def _dot_nt(a, b):
    return lax.dot_general(a, b, (((1,), (1,)), ((), ())), preferred_element_type=F32)


def _dot_tn(a, b):
    return lax.dot_general(a, b, (((0,), (0,)), ((), ())), preferred_element_type=F32)


def _sigmoid(v):
    return 1.0 / (1.0 + jnp.exp2(v * -LOG2E))


def _silu(v):
    return v * _sigmoid(v)


def _norm_matmul_kernel(has_small, x_ref, g_ref, w_ref, *rest):
    if has_small:
        ws_ref, o_ref, os_ref, h_ref = rest
    else:
        o_ref, h_ref = rest

    @pl.when(pl.program_id(1) == 0)
    def _():
        x = x_ref[...]
        ms = jnp.mean(x * x, axis=-1, keepdims=True)
        h = (x * lax.rsqrt(ms + NORM_EPS) * g_ref[...]).astype(BF16)
        h_ref[...] = h
        if has_small:
            os_ref[...] = _dot(h, ws_ref[...])

    o_ref[...] = _dot(h_ref[...], w_ref[...]).astype(o_ref.dtype)


def _norm_matmul(x2d, gain, w, w_small, tm, tn):
    t, d = x2d.shape
    n = w.shape[1]
    has_small = w_small is not None
    resident = lambda shape: pl.BlockSpec(shape, lambda i, j: (0, 0), pipeline_mode=pl.Buffered(1))
    in_specs = [pl.BlockSpec((tm, d), lambda i, j: (i, 0)), resident((1, d)),
                pl.BlockSpec((d, tn), lambda i, j: (0, j))]
    out_specs = [pl.BlockSpec((tm, tn), lambda i, j: (i, j))]
    out_shape = [jax.ShapeDtypeStruct((t, n), BF16)]
    operands = [x2d, gain, w]
    if has_small:
        ns = w_small.shape[1]
        in_specs.append(resident((d, ns)))
        out_specs.append(pl.BlockSpec((tm, ns), lambda i, j: (i, 0)))
        out_shape.append(jax.ShapeDtypeStruct((t, ns), F32))
        operands.append(w_small)
    return pl.pallas_call(
        functools.partial(_norm_matmul_kernel, has_small),
        grid=(t // tm, n // tn),
        in_specs=in_specs,
        out_specs=out_specs,
        out_shape=out_shape,
        scratch_shapes=[pltpu.VMEM((tm, d), BF16)],
        compiler_params=pltpu.CompilerParams(
            dimension_semantics=("parallel", "arbitrary"),
            vmem_limit_bytes=_norm_matmul_vmem_bytes(tm, tn, d, has_small)),
        name="norm_matmul",
    )(*operands)


def _norm_matmul_vmem_bytes(tm, tn, d, has_small):
    need = 2 * tm * d * 4 + 2 * d * tn * 2 + 2 * tm * tn * 2 + tm * d * 2 + tm * tn * 4
    if has_small:
        need += d * DT_PAD * 2 + 2 * tm * DT_PAD * 4
    return need + VMEM_COMPILER_SCRATCH_BYTES


CONV_SLAB = 256
OUT_ROWS = 32
TAIL_ROWS = 16


def _cumsum_rows(v):
    n = v.shape[0]
    row = lax.broadcasted_iota(jnp.int32, v.shape, 0)
    shift = 1
    while shift < n:
        v = v + jnp.where(row >= shift, pltpu.roll(v, shift, axis=0), 0.0)
        shift *= 2
    return v


def _split_bf16(v):
    hi = v.astype(BF16)
    lo = (v - hi.astype(F32)).astype(BF16)
    return jnp.concatenate([hi, lo], axis=1)


def _ssd_kernel(xbc_ref, z_ref, dtraw_ref, convw_ref, convb_ref, dtb_ref, alog_ref, dskip_ref,
                ng_ref, e_ref, shift_ref, y_ref, tail_ref, xs_ref, bc_ref, state_ref):
    q = SSM_CHUNK

    @pl.when(pl.program_id(1) == 0)
    def _():
        tail_ref[...] = jnp.zeros_like(tail_ref)
        state_ref[...] = jnp.zeros_like(state_ref)

    shift_mat = shift_ref[...]

    def conv_slab(s):
        c0 = s * CONV_SLAB
        cols = slice(c0, c0 + CONV_SLAB)
        u_b = xbc_ref[:, cols]
        hist = jnp.concatenate([tail_ref[:, cols], u_b], axis=0)
        w = convw_ref[:, cols]
        w_b = w.astype(BF16)
        weighted = [hist * w_b[SSM_CONV - 1 - j:SSM_CONV - j, :] for j in range(1, SSM_CONV)]
        pad_rows = shift_mat.shape[1] - (SSM_CONV - 1) * (TAIL_ROWS + q)
        weighted.append(jnp.zeros((pad_rows, CONV_SLAB), BF16))
        acc = (convb_ref[:, cols] + w[SSM_CONV - 1:SSM_CONV, :] * u_b.astype(F32)
               + _dot(shift_mat, jnp.concatenate(weighted, axis=0)))
        tail_ref[:, cols] = u_b[q - TAIL_ROWS:q, :]
        act = _silu(acc)
        if c0 < SSM_D_INNER:
            xs_ref[:, cols] = act
        else:
            bc_ref[:, c0 - SSM_D_INNER:c0 - SSM_D_INNER + CONV_SLAB] = act.astype(BF16)

    pre = dtraw_ref[...] + dtb_ref[...]
    dt = jnp.maximum(pre, 0.0) + jnp.log(1.0 + jnp.exp(-jnp.abs(pre)))
    a = -jnp.exp(alog_ref[...]) * LOG2E
    acs = _cumsum_rows(dt * a)
    acs_t = acs.T
    eacs = jnp.exp2(acs)
    dec = jnp.exp2(acs[q - 1:q, :] - acs)
    head_vals = jnp.concatenate([_split_bf16(dt), _split_bf16(eacs), _split_bf16(dec)], axis=0)

    row = lax.broadcasted_iota(jnp.int32, (q, q), 0)
    col = lax.broadcasted_iota(jnp.int32, (q, q), 1)
    causal = row >= col
    lane_head = lax.broadcasted_iota(jnp.int32, (q, SSM_GROUP_WIDTH), 1) // SSM_HEAD_DIM

    def group(g):
        lanes = slice(g * SSM_GROUP_WIDTH, (g + 1) * SSM_GROUP_WIDTH)
        e_g = e_ref[:, lanes]
        ex = _dot(head_vals, e_g)
        dt_e, eacs_e, dec_e = ex[0:q, :], ex[q:2 * q, :], ex[2 * q:3 * q, :]

        xs_g = xs_ref[:, lanes]
        xdt = xs_g * dt_e
        xdt_b = xdt.astype(BF16)
        bm_g = bc_ref[:, g * SSM_STATE:(g + 1) * SSM_STATE]
        cm_g = bc_ref[:, SSM_BC_DIM + g * SSM_STATE:SSM_BC_DIM + (g + 1) * SSM_STATE]
        cb = _dot_nt(cm_g, bm_g)

        m_parts = []
        x_parts = []
        for r in range(SSM_HEADS_PER_GROUP):
            h = g * SSM_HEADS_PER_GROUP + r
            seg = acs[:, h:h + 1] - acs_t[h:h + 1, :]
            lmat = jnp.exp2(jnp.where(causal, seg, -jnp.inf))
            m_parts.append((cb * lmat).astype(BF16))
            x_parts.append(jnp.where(lane_head == r, xdt_b, jnp.zeros_like(xdt_b)))
        m_cat = jnp.concatenate(m_parts, axis=1)
        x_bd = jnp.concatenate(x_parts, axis=0)
        y_diag = _dot(m_cat, x_bd)

        st = state_ref[g]
        y_off = _dot(cm_g, st.astype(BF16))
        xw = (xdt * dec_e).astype(BF16)
        state_ref[g] = st * eacs_e[q - 1:q, :] + _dot_tn(bm_g, xw)

        for r0 in range(0, q, OUT_ROWS):
            rr = slice(r0, r0 + OUT_ROWS)
            y = y_diag[rr, :] + y_off[rr, :] * eacs_e[rr, :] + xs_g[rr, :] * dskip_ref[:, lanes]
            y = y * _silu(z_ref[rr, lanes].astype(F32))
            ms = jnp.mean(y * y, axis=-1, keepdims=True)
            y = y * lax.rsqrt(ms + NORM_EPS) * ng_ref[:, lanes]
            y_ref[rr, lanes] = y.astype(y_ref.dtype)

    xs_slabs, bc_slabs = SSM_D_INNER // CONV_SLAB, SSM_BC_DIM // CONV_SLAB
    slab_order = []
    for k in range(max(xs_slabs, bc_slabs)):
        if k < xs_slabs:
            slab_order.append(k)
        if k < bc_slabs:
            slab_order += [xs_slabs + k, xs_slabs + bc_slabs + k]
    done = 0
    for g in range(SSM_GROUPS):
        needed = (g * SSM_GROUP_WIDTH // CONV_SLAB,
                  (SSM_D_INNER + g * SSM_STATE) // CONV_SLAB,
                  (SSM_D_INNER + SSM_BC_DIM + g * SSM_STATE) // CONV_SLAB)
        ready = 1 + max(slab_order.index(s) for s in needed)
        target = min(len(slab_order), max(ready, done) + 1)
        while done < target:
            conv_slab(slab_order[done])
            done += 1
        group(g)
    assert done == len(slab_order)


CONV_SHIFT_K = 512


def _conv_shift_matrix():
    q = SSM_CHUNK
    m = np.zeros((q, CONV_SHIFT_K), np.float32)
    for j in range(1, SSM_CONV):
        for t in range(q):
            m[t, (j - 1) * (TAIL_ROWS + q) + TAIL_ROWS + t - j] = 1.0
    return m


def _ssd(proj3, dt3, conv_w, conv_b, dt_bias, a_log, d_skip_e, norm_g, expand_mat):
    b, s, _ = proj3.shape
    q = SSM_CHUNK
    full = lambda shape: pl.BlockSpec(shape, lambda bi, ci: (0,) * len(shape))
    return pl.pallas_call(
        _ssd_kernel,
        grid=(b, s // q),
        in_specs=[
            pl.BlockSpec((None, q, SSM_CONV_DIM), lambda bi, ci: (bi, ci, OFF_XBC // SSM_CONV_DIM)),
            pl.BlockSpec((None, q, SSM_D_INNER), lambda bi, ci: (bi, ci, OFF_Z // SSM_D_INNER)),
            pl.BlockSpec((None, q, DT_PAD), lambda bi, ci: (bi, ci, 0)),
            full((SSM_CONV, SSM_CONV_DIM)),
            full((1, SSM_CONV_DIM)),
            full((1, DT_PAD)),
            full((1, DT_PAD)),
            full((1, SSM_D_INNER)),
            full((1, SSM_D_INNER)),
            full((2 * DT_PAD, SSM_D_INNER)),
            full((q, CONV_SHIFT_K)),
        ],
        out_specs=pl.BlockSpec((None, q, SSM_D_INNER), lambda bi, ci: (bi, ci, 0)),
        out_shape=jax.ShapeDtypeStruct((b, s, SSM_D_INNER), BF16),
        scratch_shapes=[
            pltpu.VMEM((TAIL_ROWS, SSM_CONV_DIM), BF16),
            pltpu.VMEM((q, SSM_D_INNER), F32),
            pltpu.VMEM((q, 2 * SSM_BC_DIM), BF16),
            pltpu.VMEM((SSM_GROUPS, SSM_STATE, SSM_GROUP_WIDTH), F32),
        ],
        compiler_params=_cparams(("parallel", "arbitrary")),
        name="ssd",
    )(proj3, proj3, dt3, conv_w, conv_b, dt_bias, a_log, d_skip_e, norm_g, expand_mat,
      jnp.asarray(_conv_shift_matrix(), BF16))


def _t5_bucket_static(rel):
    n = np.maximum(rel, 0)
    max_exact = REL_BUCKETS // 2
    nf = np.maximum(n, 1).astype(np.float64)
    large = max_exact + (np.log(nf / max_exact) / math.log(REL_MAX_DIST / max_exact)
                         * (REL_BUCKETS - max_exact)).astype(np.int64)
    large = np.minimum(large, REL_BUCKETS - 1)
    return np.where(n < max_exact, n, large)


def _near_bucket_table():
    kj = np.arange(2 * ATT_BLOCK)[:, None]
    qi = np.arange(ATT_BLOCK)[None, :]
    rel = qi + ATT_BLOCK - kj
    return np.where(rel >= 0, _t5_bucket_static(rel), -1).astype(np.int32)


def _bias_table_kernel(relb_ref, bucket_ref, o_ref):
    h = pl.program_id(0)
    bucket = bucket_ref[...]
    acc = jnp.full(bucket.shape, -jnp.inf, F32)
    for bkt in range(REL_BUCKETS):
        acc = jnp.where(bucket == bkt, relb_ref[bkt, h], acc)
    o_ref[...] = acc * LOG2E


def _bias_table(rel_bias):
    bucket = jnp.asarray(_near_bucket_table())
    return pl.pallas_call(
        _bias_table_kernel,
        grid=(DIFF_HEADS,),
        in_specs=[
            pl.BlockSpec(memory_space=pltpu.SMEM),
            pl.BlockSpec(bucket.shape, lambda h: (0, 0)),
        ],
        out_specs=pl.BlockSpec((None,) + bucket.shape, lambda h: (h, 0, 0)),
        out_shape=jax.ShapeDtypeStruct((DIFF_HEADS,) + bucket.shape, F32),
        compiler_params=_cparams(("arbitrary",)),
        name="bias_table",
    )(rel_bias, bucket)


class _AttnTile:
    def __init__(self, i, far_bias, lam, q_ref, k_ref, g_ref, tab_ref, sg_ref, o_ref, vt_ref, s_refs):
        self.i = i
        self.far_bias, self.lam = far_bias, lam
        self.k_ref, self.g_ref, self.tab_ref, self.sg_ref = k_ref, g_ref, tab_ref, sg_ref
        self.o_ref, self.vt_ref, self.s_refs = o_ref, vt_ref, s_refs
        self.rows = slice(i * ATT_BLOCK, (i + 1) * ATT_BLOCK)
        self.q_ref = q_ref
        self.qc = None
        self.mx = [None, None]
        self.probs = [[], []]

    def _scaled_queries(self):
        if self.qc is None:
            qq = self.q_ref[self.rows, :]
            lane = lax.broadcasted_iota(jnp.int32, qq.shape, 1)
            scale = DIFF_HEAD_DIM ** -0.5 * LOG2E
            self.qc = [jnp.where(keep, qq, jnp.zeros_like(qq)) * scale
                       for keep in (lane < DIFF_HEAD_DIM, lane >= DIFF_HEAD_DIM)]
        return self.qc

    def _near(self, j):
        return j >= self.i - 1

    def _score_step(self, c, j):
        blk = ATT_BLOCK
        st = _dot_nt(self.k_ref[j * blk:(j + 1) * blk, :], self._scaled_queries()[c])
        if j == self.i:
            bm = self._diag_scores(c, j, st)
        else:
            if self._near(j):
                t0 = (j - (self.i - 1)) * blk
                st = st + self.tab_ref[t0:t0 + blk, :]
            self.s_refs[c][j * blk:(j + 1) * blk, :] = st
            bm = jnp.max(st, axis=0, keepdims=True)
            if not self._near(j):
                bm = bm + self.far_bias
        self.mx[c] = bm if self.mx[c] is None else jnp.maximum(self.mx[c], bm)

    def _diag_scores(self, c, j, st):
        blk, half = ATT_BLOCK, ATT_BLOCK // 2
        t0 = (j - (self.i - 1)) * blk
        top = st[0:half, :] + self.tab_ref[t0:t0 + half, :]
        low = st[half:blk, half:blk] + self.tab_ref[t0 + half:t0 + blk, half:blk]
        self.s_refs[c][j * blk:j * blk + half, :] = top
        self.s_refs[c][j * blk + half:(j + 1) * blk, half:blk] = low
        bm_top = jnp.max(top, axis=0, keepdims=True)
        bm_low = jnp.max(low, axis=0, keepdims=True)
        return jnp.concatenate([bm_top[:, 0:half], jnp.maximum(bm_top[:, half:blk], bm_low)], axis=1)

    def _prob_step(self, c, j):
        blk = ATT_BLOCK
        m = self.mx[c] if self._near(j) else self.mx[c] - self.far_bias
        if j == self.i:
            half = blk // 2
            top = jnp.exp2(self.s_refs[c][j * blk:j * blk + half, :] - m)
            low = jnp.exp2(self.s_refs[c][j * blk + half:(j + 1) * blk, half:blk] - m[:, half:blk])
            low = jnp.concatenate([jnp.zeros((half, half), F32), low], axis=1)
            p = jnp.concatenate([top, low], axis=0)
        else:
            p = jnp.exp2(self.s_refs[c][j * blk:(j + 1) * blk, :] - m)
        self.probs[c].append(p.astype(BF16))

    def score_steps(self):
        return [functools.partial(self._score_step, c, j) for j in range(self.i + 1) for c in range(2)]

    def prob_steps(self):
        return [functools.partial(self._prob_step, c, j) for j in range(self.i + 1) for c in range(2)]

    def finish(self):
        n_keys = (self.i + 1) * ATT_BLOCK
        accs = [_dot(self.vt_ref[:, 0:n_keys], jnp.concatenate(p, axis=0)) for p in self.probs]
        parts = [a[0:DIFF_V_DIM, :] * (1.0 / a[DIFF_V_DIM:DIFF_V_DIM + 1, :]) for a in accs]
        ot = parts[0] - self.lam * parts[1]
        ms = jnp.mean(ot * ot, axis=0, keepdims=True)
        ot = ot * lax.rsqrt(ms + NORM_EPS) * self.sg_ref[...] * (1.0 - LAM_INIT)
        gate = _silu(self.g_ref[self.rows, :].astype(F32))
        self.o_ref[self.rows, :] = (ot.T * gate).astype(self.o_ref.dtype)


def _interleave(first, second):
    n1, n2 = len(first), len(second)
    i1 = i2 = 0
    while i1 < n1 or i2 < n2:
        if i2 >= n2 or (i1 < n1 and i1 * n2 <= i2 * n1):
            first[i1]()
            i1 += 1
        else:
            second[i2]()
            i2 += 1


def _diff_attn_kernel(q_ref, k_ref, v_ref, g_ref, tab_ref, lam_ref, sg_ref, o_ref,
                      vt_ref, *score_refs):
    n_tiles = k_ref.shape[0] // ATT_BLOCK

    for j in range(k_ref.shape[0] // DIFF_V_DIM):
        rows = slice(j * DIFF_V_DIM, (j + 1) * DIFF_V_DIM)
        vt_ref[0:DIFF_V_DIM, rows] = v_ref[rows, :].astype(F32).T.astype(BF16)
    vt_ref[DIFF_V_DIM:, :] = jnp.ones((ONES_ROWS, k_ref.shape[0]), BF16)

    far_bias = tab_ref[0:1, :]
    lam_p = lam_ref[...]
    lam = (jnp.exp(jnp.sum(lam_p[0:1, :] * lam_p[1:2, :], axis=-1, keepdims=True))
           - jnp.exp(jnp.sum(lam_p[2:3, :] * lam_p[3:4, :], axis=-1, keepdims=True)) + LAM_INIT)

    tiles = [_AttnTile(t, far_bias, lam, q_ref, k_ref, g_ref, tab_ref, sg_ref, o_ref, vt_ref,
                       score_refs[2 * t:2 * t + 2]) for t in range(n_tiles)]
    for step in tiles[0].score_steps():
        step()
    for t in range(n_tiles):
        nxt = tiles[t + 1].score_steps() if t + 1 < n_tiles else []
        _interleave(tiles[t].prob_steps(), nxt)
        tiles[t].finish()


def _diff_attn(proj3, table, lam_params, subln_g_cols):
    b, s, _ = proj3.shape
    blk = ATT_BLOCK
    hw = DIFF_V_DIM
    head_cols = lambda off: pl.BlockSpec((None, s, hw), lambda bi, h: (bi, 0, off // hw + h))
    score_scratch = [pltpu.VMEM(((t + 1) * blk, blk), F32)
                     for t in range(s // blk) for _ in range(2)]
    return pl.pallas_call(
        _diff_attn_kernel,
        grid=(b, DIFF_HEADS),
        in_specs=[
            head_cols(OFF_DQ), head_cols(OFF_DK), head_cols(OFF_DV), head_cols(OFF_DG),
            pl.BlockSpec((None, 2 * blk, blk), lambda bi, h: (h, 0, 0)),
            pl.BlockSpec((4, DIFF_HEAD_DIM), lambda bi, h: (0, 0)),
            pl.BlockSpec((hw, blk), lambda bi, h: (0, 0)),
        ],
        out_specs=pl.BlockSpec((None, s, hw), lambda bi, h: (bi, 0, h)),
        out_shape=jax.ShapeDtypeStruct((b, s, DIFF_WIDTH), BF16),
        scratch_shapes=[pltpu.VMEM((hw + ONES_ROWS, s), BF16)] + score_scratch,
        compiler_params=_cparams(("parallel", "parallel")),
        name="diff_attn",
    )(proj3, proj3, proj3, proj3, table, lam_params, subln_g_cols)


MEM_Q_TILE = 256


def _mem_attn_kernel(q_ref, g_ref, k_ref, v_ref, o_ref, vt_ref):
    hd = MEM_HEAD_DIM
    m_len = k_ref.shape[0]
    for r in range(m_len // 128):
        for c in range(hd // 128):
            tile = v_ref[r * 128:(r + 1) * 128, c * 128:(c + 1) * 128].astype(F32)
            vt_ref[c * 128:(c + 1) * 128, r * 128:(r + 1) * 128] = tile.T.astype(BF16)
    vt_ref[hd:, :] = jnp.ones((ONES_ROWS, m_len), BF16)
    scale = MEM_HEAD_DIM ** -0.5 * LOG2E
    kk = k_ref[...]
    for t in range(q_ref.shape[0] // MEM_Q_TILE):
        rows = slice(t * MEM_Q_TILE, (t + 1) * MEM_Q_TILE)
        st = _dot_nt(kk, q_ref[rows, :] * scale)
        p = jnp.exp2(st - jnp.max(st, axis=0, keepdims=True))
        acc = _dot(vt_ref[...], p.astype(BF16))
        ot = acc[0:hd, :] * (1.0 / acc[hd:hd + 1, :])
        o_ref[rows, :] = (ot.T * _silu(g_ref[rows, :].astype(F32))).astype(o_ref.dtype)


def _mem_attn(proj3, kv3):
    b, s, _ = proj3.shape
    m = kv3.shape[1]
    hw = MEM_HEAD_DIM
    return pl.pallas_call(
        _mem_attn_kernel,
        grid=(b, MEM_HEADS),
        in_specs=[
            pl.BlockSpec((None, s, hw), lambda bi, h: (bi, 0, OFF_MQ // hw + h)),
            pl.BlockSpec((None, s, hw), lambda bi, h: (bi, 0, OFF_MG // hw + h)),
            pl.BlockSpec((None, m, hw), lambda bi, h: (bi, 0, h)),
            pl.BlockSpec((None, m, hw), lambda bi, h: (bi, 0, MEM_HEADS + h)),
        ],
        out_specs=pl.BlockSpec((None, s, hw), lambda bi, h: (bi, 0, h)),
        out_shape=jax.ShapeDtypeStruct((b, s, MEM_WIDTH), BF16),
        scratch_shapes=[pltpu.VMEM((hw + ONES_ROWS, m), BF16)],
        compiler_params=_cparams(("parallel", "parallel")),
        name="mem_attn",
    )(proj3, proj3, kv3, kv3)


def _merge_kernel(x_ref, ys_ref, yd_ref, ym_ref, gate_ref, ws_ref, wd_ref, wm_ref, wo_ref,
                  fg_ref, o_ref):
    d = D_MODEL
    gate = gate_ref[...].astype(F32)
    merged = (_sigmoid(gate[:, 0:d]) * _dot(ys_ref[...], ws_ref[...])
              + _sigmoid(gate[:, d:2 * d]) * _dot(yd_ref[...], wd_ref[...])
              + _sigmoid(gate[:, 2 * d:3 * d]) * _dot(ym_ref[...], wm_ref[...]))
    xo = x_ref[...] + _dot(merged.astype(BF16), wo_ref[...])
    ms = jnp.mean(xo * xo, axis=-1, keepdims=True)
    o_ref[...] = xo * lax.rsqrt(ms + NORM_EPS) * fg_ref[...]


def _merge(x2d, ys, yd, ym, proj, w_s, w_d, w_m, w_o, final_g, tm):
    t, d = x2d.shape
    gw = N_BRANCHES * D_MODEL
    rows = lambda width, cb=0: pl.BlockSpec((tm, width), lambda i: (i, cb))
    whole = lambda shape: pl.BlockSpec(shape, lambda i: (0, 0))
    return pl.pallas_call(
        _merge_kernel,
        grid=(t // tm,),
        in_specs=[
            rows(d), rows(SSM_D_INNER), rows(DIFF_WIDTH), rows(MEM_WIDTH),
            rows(gw, OFF_GATE // gw),
            whole(w_s.shape), whole(w_d.shape), whole(w_m.shape), whole(w_o.shape),
            whole((1, d)),
        ],
        out_specs=rows(d),
        out_shape=jax.ShapeDtypeStruct((t, d), F32),
        compiler_params=_cparams(("parallel",)),
        name="merge",
    )(x2d, ys, yd, ym, proj, w_s, w_d, w_m, w_o, final_g)


def _head_expand_matrix():
    e = np.zeros((2 * DT_PAD, SSM_D_INNER), np.float32)
    for h in range(SSM_HEADS):
        e[h, h * SSM_HEAD_DIM:(h + 1) * SSM_HEAD_DIM] = 1.0
        e[DT_PAD + h, h * SSM_HEAD_DIM:(h + 1) * SSM_HEAD_DIM] = 1.0
    return e


def _pad_lanes(v, width):
    return jnp.pad(v.reshape(1, -1), ((0, 0), (0, width - v.shape[-1])))


def kernel(x, mem, norm_gain, w_in, conv_w, conv_b, dt_bias, a_log, d_skip, ssm_norm_gain,
           lambda_q1, lambda_k1, lambda_q2, lambda_k2, subln_gain, mem_norm_gain, w_mem_kv,
           w_br_ssm, w_br_diff, w_br_mem, w_out, rel_bias, final_norm_gain):
    b, s, d = x.shape
    t = b * s
    assert norm_gain.shape[0] == 1, "single-layer (DEPTH == 1) kernel"
    assert s % ATT_BLOCK == 0 and s % SSM_CHUNK == 0 and d == D_MODEL
    assert int(_t5_bucket_static(np.array([ATT_BLOCK]))[0]) == REL_BUCKETS - 1

    ro = _REF_OFFS
    wi = w_in[0].astype(BF16)
    w_main = jnp.concatenate([
        wi[:, ro[1]:ro[2]],
        wi[:, ro[0]:ro[1]],
        wi[:, ro[3]:ro[10]],
    ], axis=1)
    w_dt = jnp.pad(wi[:, ro[2]:ro[3]], ((0, 0), (0, DT_PAD - SSM_HEADS)))

    x2d = x.reshape(t, d)
    proj, dt_raw = _norm_matmul(x2d, norm_gain[0].reshape(1, d), w_main, w_dt,
                                min(IN_PROJ_ROWS, t), IN_PROJ_COLS)
    proj3 = proj.reshape(b, s, PROJ_DIM)
    dt3 = dt_raw.reshape(b, s, DT_PAD)

    y_ssm = _ssd(
        proj3, dt3, conv_w[0], conv_b[0].reshape(1, -1),
        _pad_lanes(dt_bias[0], DT_PAD), _pad_lanes(a_log[0], DT_PAD),
        jnp.repeat(d_skip[0], SSM_HEAD_DIM).reshape(1, -1), ssm_norm_gain[0].reshape(1, -1),
        jnp.asarray(_head_expand_matrix(), BF16))

    table = _bias_table(rel_bias)
    lam_params = jnp.stack([lambda_q1[0], lambda_k1[0], lambda_q2[0], lambda_k2[0]])
    subln_cols = jnp.broadcast_to(subln_gain[0].reshape(-1, 1), (DIFF_V_DIM, ATT_BLOCK))
    y_diff = _diff_attn(proj3, table, lam_params, subln_cols)

    m_len = mem.shape[1]
    kv, = _norm_matmul(mem.reshape(b * m_len, d), mem_norm_gain[0].reshape(1, d),
                       w_mem_kv[0].astype(BF16), None, min(1024, b * m_len), 1024)
    y_mem = _mem_attn(proj3, kv.reshape(b, m_len, 2 * MEM_WIDTH))

    out = _merge(x2d, y_ssm.reshape(t, -1), y_diff.reshape(t, -1), y_mem.reshape(t, -1), proj,
                 w_br_ssm[0].astype(BF16), w_br_diff[0].astype(BF16), w_br_mem[0].astype(BF16),
                 w_out[0].astype(BF16), final_norm_gain.reshape(1, d), min(512, t))
    return out.reshape(b, s, d)
```

```python
import functools
import math

import numpy as np
import jax
import jax.numpy as jnp
from jax import lax
from jax.experimental import pallas as pl
from jax.experimental.pallas import tpu as pltpu

F32 = jnp.float32
BF16 = jnp.bfloat16

D_MODEL = 1024
NORM_EPS = 1e-5

SSM_D_INNER = 2048
SSM_HEAD_DIM = 64
SSM_HEADS = 32
SSM_GROUPS = 8
SSM_HEADS_PER_GROUP = SSM_HEADS // SSM_GROUPS
SSM_STATE = 128
SSM_CONV = 4
SSM_CHUNK = 128
SSM_BC_DIM = SSM_GROUPS * SSM_STATE
SSM_CONV_DIM = SSM_D_INNER + 2 * SSM_BC_DIM
SSM_GROUP_WIDTH = SSM_D_INNER // SSM_GROUPS

DIFF_HEADS = 8
DIFF_HEAD_DIM = 64
DIFF_V_DIM = 128
DIFF_WIDTH = 1024
ATT_BLOCK = 256

REL_BUCKETS = 32
REL_MAX_DIST = 128

MEM_HEADS = 4
MEM_HEAD_DIM = 256
MEM_WIDTH = 1024

N_BRANCHES = 3
LAM_INIT = 0.8 - 0.6 * math.exp(-0.3 * 0)
LOG2E = math.log2(math.e)
ONES_ROWS = 16

OFF_XBC = 0
OFF_Z = OFF_XBC + SSM_CONV_DIM
OFF_DQ = OFF_Z + SSM_D_INNER
OFF_DK = OFF_DQ + DIFF_WIDTH
OFF_DV = OFF_DK + DIFF_WIDTH
OFF_DG = OFF_DV + DIFF_WIDTH
OFF_MQ = OFF_DG + DIFF_WIDTH
OFF_MG = OFF_MQ + MEM_WIDTH
OFF_GATE = OFF_MG + MEM_WIDTH
PROJ_DIM = OFF_GATE + N_BRANCHES * D_MODEL
DT_PAD = 128

_REF_SIZES = (SSM_D_INNER, SSM_CONV_DIM, SSM_HEADS, DIFF_WIDTH, DIFF_WIDTH, DIFF_WIDTH,
              DIFF_WIDTH, MEM_WIDTH, MEM_WIDTH, N_BRANCHES * D_MODEL)
_REF_OFFS = [0] + [int(v) for v in np.cumsum(_REF_SIZES)]

VMEM_LIMIT_BYTES = 48 * 1024 * 1024
VMEM_COMPILER_SCRATCH_BYTES = 2 * 1024 * 1024
IN_PROJ_ROWS, IN_PROJ_COLS = 2048, 1536


def _cparams(semantics):
    return pltpu.CompilerParams(dimension_semantics=semantics, vmem_limit_bytes=VMEM_LIMIT_BYTES)


def _dot(a, b):
    return jnp.dot(a, b, preferred_element_type=F32)


def _dot_nt(a, b):
    return lax.dot_general(a, b, (((1,), (1,)), ((), ())), preferred_element_type=F32)


def _dot_tn(a, b):
    return lax.dot_general(a, b, (((0,), (0,)), ((), ())), preferred_element_type=F32)


def _sigmoid(v):
    return 1.0 / (1.0 + jnp.exp2(v * -LOG2E))


def _silu(v):
    return v * _sigmoid(v)


def _norm_matmul_kernel(has_small, x_ref, g_ref, w_ref, *rest):
    if has_small:
        ws_ref, o_ref, os_ref, h_ref = rest
    else:
        o_ref, h_ref = rest

    @pl.when(pl.program_id(1) == 0)
    def _():
        x = x_ref[...]
        ms = jnp.mean(x * x, axis=-1, keepdims=True)
        h = (x * lax.rsqrt(ms + NORM_EPS) * g_ref[...]).astype(BF16)
        h_ref[...] = h
        if has_small:
            os_ref[...] = _dot(h, ws_ref[...])

    o_ref[...] = _dot(h_ref[...], w_ref[...]).astype(o_ref.dtype)


def _norm_matmul(x2d, gain, w, w_small, tm, tn):
    t, d = x2d.shape
    n = w.shape[1]
    has_small = w_small is not None
    resident = lambda shape: pl.BlockSpec(shape, lambda i, j: (0, 0), pipeline_mode=pl.Buffered(1))
    in_specs = [pl.BlockSpec((tm, d), lambda i, j: (i, 0)), resident((1, d)),
                pl.BlockSpec((d, tn), lambda i, j: (0, j))]
    out_specs = [pl.BlockSpec((tm, tn), lambda i, j: (i, j))]
    out_shape = [jax.ShapeDtypeStruct((t, n), BF16)]
    operands = [x2d, gain, w]
    if has_small:
        ns = w_small.shape[1]
        in_specs.append(resident((d, ns)))
        out_specs.append(pl.BlockSpec((tm, ns), lambda i, j: (i, 0)))
        out_shape.append(jax.ShapeDtypeStruct((t, ns), F32))
        operands.append(w_small)
    return pl.pallas_call(
        functools.partial(_norm_matmul_kernel, has_small),
        grid=(t // tm, n // tn),
        in_specs=in_specs,
        out_specs=out_specs,
        out_shape=out_shape,
        scratch_shapes=[pltpu.VMEM((tm, d), BF16)],
        compiler_params=pltpu.CompilerParams(
            dimension_semantics=("parallel", "arbitrary"),
            vmem_limit_bytes=_norm_matmul_vmem_bytes(tm, tn, d, has_small)),
        name="norm_matmul",
    )(*operands)


def _norm_matmul_vmem_bytes(tm, tn, d, has_small):
    need = 2 * tm * d * 4 + 2 * d * tn * 2 + 2 * tm * tn * 2 + tm * d * 2 + tm * tn * 4
    if has_small:
        need += d * DT_PAD * 2 + 2 * tm * DT_PAD * 4
    return need + VMEM_COMPILER_SCRATCH_BYTES


CONV_SLAB = 256
SSD_CHUNKS_PER_STEP = 4
OUT_ROWS = 32
TAIL_ROWS = 16


def _cumsum_rows(v):
    n = v.shape[0]
    row = lax.broadcasted_iota(jnp.int32, v.shape, 0)
    shift = 1
    while shift < n:
        v = v + jnp.where(row >= shift, pltpu.roll(v, shift, axis=0), 0.0)
        shift *= 2
    return v


def _split_bf16(v):
    hi = v.astype(BF16)
    lo = (v - hi.astype(F32)).astype(BF16)
    return jnp.concatenate([hi, lo], axis=1)


def _ssd_chunk(xbc_ref, z_ref, dtraw_ref, convw_ref, convb_ref, dtb_ref, alog_ref, dskip_ref,
               ng_ref, e_ref, shift_ref, y_ref, tail_ref, xs_ref, bc_ref, state_ref):
    q = SSM_CHUNK

    shift_mat = shift_ref[...]

    def conv_slab(s):
        c0 = s * CONV_SLAB
        cols = slice(c0, c0 + CONV_SLAB)
        u_b = xbc_ref[:, cols]
        hist = jnp.concatenate([tail_ref[:, cols], u_b], axis=0)
        w = convw_ref[:, cols]
        w_b = w.astype(BF16)
        weighted = [hist * w_b[SSM_CONV - 1 - j:SSM_CONV - j, :] for j in range(1, SSM_CONV)]
        pad_rows = shift_mat.shape[1] - (SSM_CONV - 1) * (TAIL_ROWS + q)
        weighted.append(jnp.zeros((pad_rows, CONV_SLAB), BF16))
        acc = (convb_ref[:, cols] + w[SSM_CONV - 1:SSM_CONV, :] * u_b.astype(F32)
               + _dot(shift_mat, jnp.concatenate(weighted, axis=0)))
        tail_ref[:, cols] = u_b[q - TAIL_ROWS:q, :]
        act = _silu(acc)
        if c0 < SSM_D_INNER:
            xs_ref[:, cols] = act
        else:
            bc_ref[:, c0 - SSM_D_INNER:c0 - SSM_D_INNER + CONV_SLAB] = act.astype(BF16)

    pre = dtraw_ref[...] + dtb_ref[...]
    dt = jnp.maximum(pre, 0.0) + jnp.log(1.0 + jnp.exp(-jnp.abs(pre)))
    a = -jnp.exp(alog_ref[...]) * LOG2E
    acs = _cumsum_rows(dt * a)
    acs_t = acs.T
    eacs = jnp.exp2(acs)
    dec = jnp.exp2(acs[q - 1:q, :] - acs)
    head_vals = jnp.concatenate([_split_bf16(dt), _split_bf16(eacs), _split_bf16(dec)], axis=0)

    row = lax.broadcasted_iota(jnp.int32, (q, q), 0)
    col = lax.broadcasted_iota(jnp.int32, (q, q), 1)
    causal = row >= col
    lane_head = lax.broadcasted_iota(jnp.int32, (q, SSM_GROUP_WIDTH), 1) // SSM_HEAD_DIM

    def group(g):
        lanes = slice(g * SSM_GROUP_WIDTH, (g + 1) * SSM_GROUP_WIDTH)
        e_g = e_ref[:, lanes]
        ex = _dot(head_vals, e_g)
        dt_e, eacs_e, dec_e = ex[0:q, :], ex[q:2 * q, :], ex[2 * q:3 * q, :]

        xs_g = xs_ref[:, lanes]
        xdt = xs_g * dt_e
        xdt_b = xdt.astype(BF16)
        bm_g = bc_ref[:, g * SSM_STATE:(g + 1) * SSM_STATE]
        cm_g = bc_ref[:, SSM_BC_DIM + g * SSM_STATE:SSM_BC_DIM + (g + 1) * SSM_STATE]
        cb = _dot_nt(cm_g, bm_g)

        m_parts = []
        x_parts = []
        for r in range(SSM_HEADS_PER_GROUP):
            h = g * SSM_HEADS_PER_GROUP + r
            seg = acs[:, h:h + 1] - acs_t[h:h + 1, :]
            lmat = jnp.exp2(jnp.where(causal, seg, -jnp.inf))
            m_parts.append((cb * lmat).astype(BF16))
            x_parts.append(jnp.where(lane_head == r, xdt_b, jnp.zeros_like(xdt_b)))
        m_cat = jnp.concatenate(m_parts, axis=1)
        x_bd = jnp.concatenate(x_parts, axis=0)
        y_diag = _dot(m_cat, x_bd)

        st = state_ref[g]
        y_off = _dot(cm_g, st.astype(BF16))
        xw = (xdt * dec_e).astype(BF16)
        state_ref[g] = st * eacs_e[q - 1:q, :] + _dot_tn(bm_g, xw)

        for r0 in range(0, q, OUT_ROWS):
            rr = slice(r0, r0 + OUT_ROWS)
            y = y_diag[rr, :] + y_off[rr, :] * eacs_e[rr, :] + xs_g[rr, :] * dskip_ref[:, lanes]
            y = y * _silu(z_ref[rr, lanes].astype(F32))
            ms = jnp.mean(y * y, axis=-1, keepdims=True)
            y = y * lax.rsqrt(ms + NORM_EPS) * ng_ref[:, lanes]
            y_ref[rr, lanes] = y.astype(y_ref.dtype)

    xs_slabs, bc_slabs = SSM_D_INNER // CONV_SLAB, SSM_BC_DIM // CONV_SLAB
    slab_order = []
    for k in range(max(xs_slabs, bc_slabs)):
        if k < xs_slabs:
            slab_order.append(k)
        if k < bc_slabs:
            slab_order += [xs_slabs + k, xs_slabs + bc_slabs + k]
    done = 0
    for g in range(SSM_GROUPS):
        needed = (g * SSM_GROUP_WIDTH // CONV_SLAB,
                  (SSM_D_INNER + g * SSM_STATE) // CONV_SLAB,
                  (SSM_D_INNER + SSM_BC_DIM + g * SSM_STATE) // CONV_SLAB)
        ready = 1 + max(slab_order.index(s) for s in needed)
        target = min(len(slab_order), max(ready, done) + 1)
        while done < target:
            conv_slab(slab_order[done])
            done += 1
        group(g)
    assert done == len(slab_order)


CONV_SHIFT_K = 512


def _ssd_kernel(xbc_ref, z_ref, dtraw_ref, convw_ref, convb_ref, dtb_ref, alog_ref, dskip_ref,
                ng_ref, e_ref, shift_ref, y_ref, tail_ref, state_ref, *act_refs):
    @pl.when(pl.program_id(1) == 0)
    def _():
        tail_ref[...] = jnp.zeros_like(tail_ref)
        state_ref[...] = jnp.zeros_like(state_ref)

    for c in range(SSD_CHUNKS_PER_STEP):
        rows = pl.ds(c * SSM_CHUNK, SSM_CHUNK)
        _ssd_chunk(xbc_ref.at[rows], z_ref.at[rows], dtraw_ref.at[rows], convw_ref, convb_ref,
                   dtb_ref, alog_ref, dskip_ref, ng_ref, e_ref, shift_ref, y_ref.at[rows],
                   tail_ref, act_refs[2 * c], act_refs[2 * c + 1], state_ref)


def _conv_shift_matrix():
    q = SSM_CHUNK
    m = np.zeros((q, CONV_SHIFT_K), np.float32)
    for j in range(1, SSM_CONV):
        for t in range(q):
            m[t, (j - 1) * (TAIL_ROWS + q) + TAIL_ROWS + t - j] = 1.0
    return m


def _ssd(proj3, dt3, conv_w, conv_b, dt_bias, a_log, d_skip_e, norm_g, expand_mat):
    b, s, _ = proj3.shape
    q = SSD_CHUNKS_PER_STEP * SSM_CHUNK
    full = lambda shape: pl.BlockSpec(shape, lambda bi, ci: (0,) * len(shape))
    return pl.pallas_call(
        _ssd_kernel,
        grid=(b, s // q),
        in_specs=[
            pl.BlockSpec((None, q, SSM_CONV_DIM), lambda bi, ci: (bi, ci, OFF_XBC // SSM_CONV_DIM)),
            pl.BlockSpec((None, q, SSM_D_INNER), lambda bi, ci: (bi, ci, OFF_Z // SSM_D_INNER)),
            pl.BlockSpec((None, q, DT_PAD), lambda bi, ci: (bi, ci, 0)),
            full((SSM_CONV, SSM_CONV_DIM)),
            full((1, SSM_CONV_DIM)),
            full((1, DT_PAD)),
            full((1, DT_PAD)),
            full((1, SSM_D_INNER)),
            full((1, SSM_D_INNER)),
            full((2 * DT_PAD, SSM_D_INNER)),
            full((SSM_CHUNK, CONV_SHIFT_K)),
        ],
        out_specs=pl.BlockSpec((None, q, SSM_D_INNER), lambda bi, ci: (bi, ci, 0)),
        out_shape=jax.ShapeDtypeStruct((b, s, SSM_D_INNER), BF16),
        scratch_shapes=[
            pltpu.VMEM((TAIL_ROWS, SSM_CONV_DIM), BF16),
            pltpu.VMEM((SSM_GROUPS, SSM_STATE, SSM_GROUP_WIDTH), F32),
        ] + [pltpu.VMEM((SSM_CHUNK, SSM_D_INNER), F32),
             pltpu.VMEM((SSM_CHUNK, 2 * SSM_BC_DIM), BF16)] * SSD_CHUNKS_PER_STEP,
        compiler_params=_cparams(("parallel", "arbitrary")),
        name="ssd",
    )(proj3, proj3, dt3, conv_w, conv_b, dt_bias, a_log, d_skip_e, norm_g, expand_mat,
      jnp.asarray(_conv_shift_matrix(), BF16))


def _t5_bucket_static(rel):
    n = np.maximum(rel, 0)
    max_exact = REL_BUCKETS // 2
    nf = np.maximum(n, 1).astype(np.float64)
    large = max_exact + (np.log(nf / max_exact) / math.log(REL_MAX_DIST / max_exact)
                         * (REL_BUCKETS - max_exact)).astype(np.int64)
    large = np.minimum(large, REL_BUCKETS - 1)
    return np.where(n < max_exact, n, large)


def _near_bucket_table():
    kj = np.arange(2 * ATT_BLOCK)[:, None]
    qi = np.arange(ATT_BLOCK)[None, :]
    rel = qi + ATT_BLOCK - kj
    return np.where(rel >= 0, _t5_bucket_static(rel), -1).astype(np.int32)


def _bias_table_kernel(relb_ref, bucket_ref, o_ref):
    h = pl.program_id(0)
    bucket = bucket_ref[...]
    acc = jnp.full(bucket.shape, -jnp.inf, F32)
    for bkt in range(REL_BUCKETS):
        acc = jnp.where(bucket == bkt, relb_ref[bkt, h], acc)
    o_ref[...] = acc * LOG2E


def _bias_table(rel_bias):
    bucket = jnp.asarray(_near_bucket_table())
    return pl.pallas_call(
        _bias_table_kernel,
        grid=(DIFF_HEADS,),
        in_specs=[
            pl.BlockSpec(memory_space=pltpu.SMEM),
            pl.BlockSpec(bucket.shape, lambda h: (0, 0)),
        ],
        out_specs=pl.BlockSpec((None,) + bucket.shape, lambda h: (h, 0, 0)),
        out_shape=jax.ShapeDtypeStruct((DIFF_HEADS,) + bucket.shape, F32),
        compiler_params=_cparams(("arbitrary",)),
        name="bias_table",
    )(rel_bias, bucket)


class _AttnTile:
    def __init__(self, i, far_bias, lam, q_ref, k_ref, g_ref, tab_ref, sg_ref, o_ref, vt_ref, s_refs):
        self.i = i
        self.far_bias, self.lam = far_bias, lam
        self.k_ref, self.g_ref, self.tab_ref, self.sg_ref = k_ref, g_ref, tab_ref, sg_ref
        self.o_ref, self.vt_ref, self.s_refs = o_ref, vt_ref, s_refs
        self.rows = slice(i * ATT_BLOCK, (i + 1) * ATT_BLOCK)
        self.q_ref = q_ref
        self.qc = None
        self.mx = [None, None]
        self.probs = [[], []]

    def _scaled_queries(self):
        if self.qc is None:
            qq = self.q_ref[self.rows, :]
            lane = lax.broadcasted_iota(jnp.int32, qq.shape, 1)
            scale = DIFF_HEAD_DIM ** -0.5 * LOG2E
            self.qc = [jnp.where(keep, qq, jnp.zeros_like(qq)) * scale
                       for keep in (lane < DIFF_HEAD_DIM, lane >= DIFF_HEAD_DIM)]
        return self.qc

    def _near(self, j):
        return j >= self.i - 1

    def _score_step(self, c, j):
        blk = ATT_BLOCK
        st = _dot_nt(self.k_ref[j * blk:(j + 1) * blk, :], self._scaled_queries()[c])
        if j == self.i:
            bm = self._diag_scores(c, j, st)
        else:
            if self._near(j):
                t0 = (j - (self.i - 1)) * blk
                st = st + self.tab_ref[t0:t0 + blk, :]
            self.s_refs[c][j * blk:(j + 1) * blk, :] = st
            bm = jnp.max(st, axis=0, keepdims=True)
            if not self._near(j):
                bm = bm + self.far_bias
        self.mx[c] = bm if self.mx[c] is None else jnp.maximum(self.mx[c], bm)

    def _diag_scores(self, c, j, st):
        blk, half = ATT_BLOCK, ATT_BLOCK // 2
        t0 = (j - (self.i - 1)) * blk
        top = st[0:half, :] + self.tab_ref[t0:t0 + half, :]
        low = st[half:blk, half:blk] + self.tab_ref[t0 + half:t0 + blk, half:blk]
        self.s_refs[c][j * blk:j * blk + half, :] = top
        self.s_refs[c][j * blk + half:(j + 1) * blk, half:blk] = low
        bm_top = jnp.max(top, axis=0, keepdims=True)
        bm_low = jnp.max(low, axis=0, keepdims=True)
        return jnp.concatenate([bm_top[:, 0:half], jnp.maximum(bm_top[:, half:blk], bm_low)], axis=1)

    def _prob_step(self, c, j):
        blk = ATT_BLOCK
        m = self.mx[c] if self._near(j) else self.mx[c] - self.far_bias
        if j == self.i:
            half = blk // 2
            top = jnp.exp2(self.s_refs[c][j * blk:j * blk + half, :] - m)
            low = jnp.exp2(self.s_refs[c][j * blk + half:(j + 1) * blk, half:blk] - m[:, half:blk])
            low = jnp.concatenate([jnp.zeros((half, half), F32), low], axis=1)
            p = jnp.concatenate([top, low], axis=0)
        else:
            p = jnp.exp2(self.s_refs[c][j * blk:(j + 1) * blk, :] - m)
        self.probs[c].append(p.astype(BF16))

    def score_steps(self):
        return [functools.partial(self._score_step, c, j) for j in range(self.i + 1) for c in range(2)]

    def prob_steps(self):
        return [functools.partial(self._prob_step, c, j) for j in range(self.i + 1) for c in range(2)]

    def finish(self):
        n_keys = (self.i + 1) * ATT_BLOCK
        accs = [_dot(self.vt_ref[:, 0:n_keys], jnp.concatenate(p, axis=0)) for p in self.probs]
        parts = [a[0:DIFF_V_DIM, :] * (1.0 / a[DIFF_V_DIM:DIFF_V_DIM + 1, :]) for a in accs]
        ot = parts[0] - self.lam * parts[1]
        ms = jnp.mean(ot * ot, axis=0, keepdims=True)
        ot = ot * lax.rsqrt(ms + NORM_EPS) * self.sg_ref[...] * (1.0 - LAM_INIT)
        gate = _silu(self.g_ref[self.rows, :].astype(F32))
        self.o_ref[self.rows, :] = (ot.T * gate).astype(self.o_ref.dtype)


def _interleave(first, second):
    n1, n2 = len(first), len(second)
    i1 = i2 = 0
    while i1 < n1 or i2 < n2:
        if i2 >= n2 or (i1 < n1 and i1 * n2 <= i2 * n1):
            first[i1]()
            i1 += 1
        else:
            second[i2]()
            i2 += 1


def _diff_attn_kernel(q_ref, k_ref, v_ref, g_ref, tab_ref, lam_ref, sg_ref, o_ref,
                      vt_ref, *score_refs):
    n_tiles = k_ref.shape[0] // ATT_BLOCK

    for j in range(k_ref.shape[0] // DIFF_V_DIM):
        rows = slice(j * DIFF_V_DIM, (j + 1) * DIFF_V_DIM)
        vt_ref[0:DIFF_V_DIM, rows] = v_ref[rows, :].astype(F32).T.astype(BF16)
    vt_ref[DIFF_V_DIM:, :] = jnp.ones((ONES_ROWS, k_ref.shape[0]), BF16)

    far_bias = tab_ref[0:1, :]
    lam_p = lam_ref[...]
    lam = (jnp.exp(jnp.sum(lam_p[0:1, :] * lam_p[1:2, :], axis=-1, keepdims=True))
           - jnp.exp(jnp.sum(lam_p[2:3, :] * lam_p[3:4, :], axis=-1, keepdims=True)) + LAM_INIT)

    tiles = [_AttnTile(t, far_bias, lam, q_ref, k_ref, g_ref, tab_ref, sg_ref, o_ref, vt_ref,
                       score_refs[2 * t:2 * t + 2]) for t in range(n_tiles)]
    for step in tiles[0].score_steps():
        step()
    for t in range(n_tiles):
        nxt = tiles[t + 1].score_steps() if t + 1 < n_tiles else []
        _interleave(tiles[t].prob_steps(), nxt)
        tiles[t].finish()


def _diff_attn(proj3, table, lam_params, subln_g_cols):
    b, s, _ = proj3.shape
    blk = ATT_BLOCK
    hw = DIFF_V_DIM
    head_cols = lambda off: pl.BlockSpec((None, s, hw), lambda bi, h: (bi, 0, off // hw + h))
    score_scratch = [pltpu.VMEM(((t + 1) * blk, blk), F32)
                     for t in range(s // blk) for _ in range(2)]
    return pl.pallas_call(
        _diff_attn_kernel,
        grid=(b, DIFF_HEADS),
        in_specs=[
            head_cols(OFF_DQ), head_cols(OFF_DK), head_cols(OFF_DV), head_cols(OFF_DG),
            pl.BlockSpec((None, 2 * blk, blk), lambda bi, h: (h, 0, 0)),
            pl.BlockSpec((4, DIFF_HEAD_DIM), lambda bi, h: (0, 0)),
            pl.BlockSpec((hw, blk), lambda bi, h: (0, 0)),
        ],
        out_specs=pl.BlockSpec((None, s, hw), lambda bi, h: (bi, 0, h)),
        out_shape=jax.ShapeDtypeStruct((b, s, DIFF_WIDTH), BF16),
        scratch_shapes=[pltpu.VMEM((hw + ONES_ROWS, s), BF16)] + score_scratch,
        compiler_params=_cparams(("parallel", "parallel")),
        name="diff_attn",
    )(proj3, proj3, proj3, proj3, table, lam_params, subln_g_cols)


MEM_Q_TILE = 256


def _mem_attn_kernel(q_ref, g_ref, k_ref, v_ref, o_ref, vt_ref):
    hd = MEM_HEAD_DIM
    m_len = k_ref.shape[0]
    for r in range(m_len // 128):
        for c in range(hd // 128):
            tile = v_ref[r * 128:(r + 1) * 128, c * 128:(c + 1) * 128].astype(F32)
            vt_ref[c * 128:(c + 1) * 128, r * 128:(r + 1) * 128] = tile.T.astype(BF16)
    vt_ref[hd:, :] = jnp.ones((ONES_ROWS, m_len), BF16)
    scale = MEM_HEAD_DIM ** -0.5 * LOG2E
    kk = k_ref[...]
    for t in range(q_ref.shape[0] // MEM_Q_TILE):
        rows = slice(t * MEM_Q_TILE, (t + 1) * MEM_Q_TILE)
        st = _dot_nt(kk, q_ref[rows, :] * scale)
        p = jnp.exp2(st - jnp.max(st, axis=0, keepdims=True))
        acc = _dot(vt_ref[...], p.astype(BF16))
        ot = acc[0:hd, :] * (1.0 / acc[hd:hd + 1, :])
        o_ref[rows, :] = (ot.T * _silu(g_ref[rows, :].astype(F32))).astype(o_ref.dtype)


def _mem_attn(proj3, kv3):
    b, s, _ = proj3.shape
    m = kv3.shape[1]
    hw = MEM_HEAD_DIM
    return pl.pallas_call(
        _mem_attn_kernel,
        grid=(b, MEM_HEADS),
        in_specs=[
            pl.BlockSpec((None, s, hw), lambda bi, h: (bi, 0, OFF_MQ // hw + h)),
            pl.BlockSpec((None, s, hw), lambda bi, h: (bi, 0, OFF_MG // hw + h)),
            pl.BlockSpec((None, m, hw), lambda bi, h: (bi, 0, h)),
            pl.BlockSpec((None, m, hw), lambda bi, h: (bi, 0, MEM_HEADS + h)),
        ],
        out_specs=pl.BlockSpec((None, s, hw), lambda bi, h: (bi, 0, h)),
        out_shape=jax.ShapeDtypeStruct((b, s, MEM_WIDTH), BF16),
        scratch_shapes=[pltpu.VMEM((hw + ONES_ROWS, m), BF16)],
        compiler_params=_cparams(("parallel", "parallel")),
        name="mem_attn",
    )(proj3, proj3, kv3, kv3)


def _merge_kernel(x_ref, ys_ref, yd_ref, ym_ref, gate_ref, ws_ref, wd_ref, wm_ref, wo_ref,
                  fg_ref, o_ref):
    d = D_MODEL
    gate = gate_ref[...].astype(F32)
    merged = (_sigmoid(gate[:, 0:d]) * _dot(ys_ref[...], ws_ref[...])
              + _sigmoid(gate[:, d:2 * d]) * _dot(yd_ref[...], wd_ref[...])
              + _sigmoid(gate[:, 2 * d:3 * d]) * _dot(ym_ref[...], wm_ref[...]))
    xo = x_ref[...] + _dot(merged.astype(BF16), wo_ref[...])
    ms = jnp.mean(xo * xo, axis=-1, keepdims=True)
    o_ref[...] = xo * lax.rsqrt(ms + NORM_EPS) * fg_ref[...]


def _merge(x2d, ys, yd, ym, proj, w_s, w_d, w_m, w_o, final_g, tm):
    t, d = x2d.shape
    gw = N_BRANCHES * D_MODEL
    rows = lambda width, cb=0: pl.BlockSpec((tm, width), lambda i: (i, cb))
    whole = lambda shape: pl.BlockSpec(shape, lambda i: (0, 0))
    return pl.pallas_call(
        _merge_kernel,
        grid=(t // tm,),
        in_specs=[
            rows(d), rows(SSM_D_INNER), rows(DIFF_WIDTH), rows(MEM_WIDTH),
            rows(gw, OFF_GATE // gw),
            whole(w_s.shape), whole(w_d.shape), whole(w_m.shape), whole(w_o.shape),
            whole((1, d)),
        ],
        out_specs=rows(d),
        out_shape=jax.ShapeDtypeStruct((t, d), F32),
        compiler_params=_cparams(("parallel",)),
        name="merge",
    )(x2d, ys, yd, ym, proj, w_s, w_d, w_m, w_o, final_g)


def _head_expand_matrix():
    e = np.zeros((2 * DT_PAD, SSM_D_INNER), np.float32)
    for h in range(SSM_HEADS):
        e[h, h * SSM_HEAD_DIM:(h + 1) * SSM_HEAD_DIM] = 1.0
        e[DT_PAD + h, h * SSM_HEAD_DIM:(h + 1) * SSM_HEAD_DIM] = 1.0
    return e


def _pad_lanes(v, width):
    return jnp.pad(v.reshape(1, -1), ((0, 0), (0, width - v.shape[-1])))


def kernel(x, mem, norm_gain, w_in, conv_w, conv_b, dt_bias, a_log, d_skip, ssm_norm_gain,
           lambda_q1, lambda_k1, lambda_q2, lambda_k2, subln_gain, mem_norm_gain, w_mem_kv,
           w_br_ssm, w_br_diff, w_br_mem, w_out, rel_bias, final_norm_gain):
    b, s, d = x.shape
    t = b * s
    assert norm_gain.shape[0] == 1, "single-layer (DEPTH == 1) kernel"
    assert s % ATT_BLOCK == 0 and s % (SSD_CHUNKS_PER_STEP * SSM_CHUNK) == 0 and d == D_MODEL
    assert int(_t5_bucket_static(np.array([ATT_BLOCK]))[0]) == REL_BUCKETS - 1

    ro = _REF_OFFS
    wi = w_in[0].astype(BF16)
    w_main = jnp.concatenate([
        wi[:, ro[1]:ro[2]],
        wi[:, ro[0]:ro[1]],
        wi[:, ro[3]:ro[10]],
    ], axis=1)
    w_dt = jnp.pad(wi[:, ro[2]:ro[3]], ((0, 0), (0, DT_PAD - SSM_HEADS)))

    x2d = x.reshape(t, d)
    proj, dt_raw = _norm_matmul(x2d, norm_gain[0].reshape(1, d), w_main, w_dt,
                                min(IN_PROJ_ROWS, t), IN_PROJ_COLS)
    proj3 = proj.reshape(b, s, PROJ_DIM)
    dt3 = dt_raw.reshape(b, s, DT_PAD)

    y_ssm = _ssd(
        proj3, dt3, conv_w[0], conv_b[0].reshape(1, -1),
        _pad_lanes(dt_bias[0], DT_PAD), _pad_lanes(a_log[0], DT_PAD),
        jnp.repeat(d_skip[0], SSM_HEAD_DIM).reshape(1, -1), ssm_norm_gain[0].reshape(1, -1),
        jnp.asarray(_head_expand_matrix(), BF16))

    table = _bias_table(rel_bias)
    lam_params = jnp.stack([lambda_q1[0], lambda_k1[0], lambda_q2[0], lambda_k2[0]])
    subln_cols = jnp.broadcast_to(subln_gain[0].reshape(-1, 1), (DIFF_V_DIM, ATT_BLOCK))
    y_diff = _diff_attn(proj3, table, lam_params, subln_cols)

    m_len = mem.shape[1]
    kv, = _norm_matmul(mem.reshape(b * m_len, d), mem_norm_gain[0].reshape(1, d),
                       w_mem_kv[0].astype(BF16), None, min(1024, b * m_len), 1024)
    y_mem = _mem_attn(proj3, kv.reshape(b, m_len, 2 * MEM_WIDTH))

    out = _merge(x2d, y_ssm.reshape(t, -1), y_diff.reshape(t, -1), y_mem.reshape(t, -1), proj,
                 w_br_ssm[0].astype(BF16), w_br_diff[0].astype(BF16), w_br_mem[0].astype(BF16),
                 w_out[0].astype(BF16), final_norm_gain.reshape(1, d), min(512, t))
    return out.reshape(b, s, d)
```

```python
import functools
import math

import numpy as np
import jax
import jax.numpy as jnp
from jax import lax
from jax.experimental import pallas as pl
from jax.experimental.pallas import tpu as pltpu

F32 = jnp.float32
BF16 = jnp.bfloat16

D_MODEL = 1024
NORM_EPS = 1e-5

SSM_D_INNER = 2048
SSM_HEAD_DIM = 64
SSM_HEADS = 32
SSM_GROUPS = 8
SSM_HEADS_PER_GROUP = SSM_HEADS // SSM_GROUPS
SSM_STATE = 128
SSM_CONV = 4
SSM_CHUNK = 128
SSM_BC_DIM = SSM_GROUPS * SSM_STATE
SSM_CONV_DIM = SSM_D_INNER + 2 * SSM_BC_DIM
SSM_GROUP_WIDTH = SSM_D_INNER // SSM_GROUPS

DIFF_HEADS = 8
DIFF_HEAD_DIM = 64
DIFF_V_DIM = 128
DIFF_WIDTH = 1024
ATT_BLOCK = 256

REL_BUCKETS = 32
REL_MAX_DIST = 128

MEM_HEADS = 4
MEM_HEAD_DIM = 256
MEM_WIDTH = 1024

N_BRANCHES = 3
LAM_INIT = 0.8 - 0.6 * math.exp(-0.3 * 0)
LOG2E = math.log2(math.e)
ONES_ROWS = 16

OFF_XBC = 0
OFF_Z = OFF_XBC + SSM_CONV_DIM
OFF_DQ = OFF_Z + SSM_D_INNER
OFF_DK = OFF_DQ + DIFF_WIDTH
OFF_DV = OFF_DK + DIFF_WIDTH
OFF_DG = OFF_DV + DIFF_WIDTH
OFF_MQ = OFF_DG + DIFF_WIDTH
OFF_MG = OFF_MQ + MEM_WIDTH
OFF_GATE = OFF_MG + MEM_WIDTH
PROJ_DIM = OFF_GATE + N_BRANCHES * D_MODEL
DT_PAD = 128

_REF_SIZES = (SSM_D_INNER, SSM_CONV_DIM, SSM_HEADS, DIFF_WIDTH, DIFF_WIDTH, DIFF_WIDTH,
              DIFF_WIDTH, MEM_WIDTH, MEM_WIDTH, N_BRANCHES * D_MODEL)
_REF_OFFS = [0] + [int(v) for v in np.cumsum(_REF_SIZES)]

VMEM_LIMIT_BYTES = 48 * 1024 * 1024
VMEM_COMPILER_SCRATCH_BYTES = 2 * 1024 * 1024
IN_PROJ_ROWS, IN_PROJ_COLS = 1024, 3072


def _cparams(semantics):
    return pltpu.CompilerParams(dimension_semantics=semantics, vmem_limit_bytes=VMEM_LIMIT_BYTES)


def _dot(a, b):
    return jnp.dot(a, b, preferred_element_type=F32)


def _dot_nt(a, b):
    return lax.dot_general(a, b, (((1,), (1,)), ((), ())), preferred_element_type=F32)


def _dot_tn(a, b):
    return lax.dot_general(a, b, (((0,), (0,)), ((), ())), preferred_element_type=F32)


def _sigmoid(v):
    return 1.0 / (1.0 + jnp.exp2(v * -LOG2E))


def _silu(v):
    return v * _sigmoid(v)


def _norm_matmul_kernel(has_small, x_ref, g_ref, w_ref, *rest):
    if has_small:
        ws_ref, o_ref, os_ref, h_ref = rest
    else:
        o_ref, h_ref = rest

    @pl.when(pl.program_id(1) == 0)
    def _():
        x = x_ref[...]
        ms = jnp.mean(x * x, axis=-1, keepdims=True)
        h = (x * lax.rsqrt(ms + NORM_EPS) * g_ref[...]).astype(BF16)
        h_ref[...] = h
        if has_small:
            os_ref[...] = _dot(h, ws_ref[...])

    o_ref[...] = _dot(h_ref[...], w_ref[...]).astype(o_ref.dtype)


def _norm_matmul(x2d, gain, w, w_small, tm, tn):
    t, d = x2d.shape
    n = w.shape[1]
    has_small = w_small is not None
    resident = lambda shape: pl.BlockSpec(shape, lambda i, j: (0, 0), pipeline_mode=pl.Buffered(1))
    in_specs = [pl.BlockSpec((tm, d), lambda i, j: (i, 0)), resident((1, d)),
                pl.BlockSpec((d, tn), lambda i, j: (0, j))]
    out_specs = [pl.BlockSpec((tm, tn), lambda i, j: (i, j))]
    out_shape = [jax.ShapeDtypeStruct((t, n), BF16)]
    operands = [x2d, gain, w]
    if has_small:
        ns = w_small.shape[1]
        in_specs.append(resident((d, ns)))
        out_specs.append(pl.BlockSpec((tm, ns), lambda i, j: (i, 0)))
        out_shape.append(jax.ShapeDtypeStruct((t, ns), F32))
        operands.append(w_small)
    return pl.pallas_call(
        functools.partial(_norm_matmul_kernel, has_small),
        grid=(t // tm, n // tn),
        in_specs=in_specs,
        out_specs=out_specs,
        out_shape=out_shape,
        scratch_shapes=[pltpu.VMEM((tm, d), BF16)],
        compiler_params=pltpu.CompilerParams(
            dimension_semantics=("parallel", "arbitrary"),
            vmem_limit_bytes=_norm_matmul_vmem_bytes(tm, tn, d, has_small)),
        name="norm_matmul",
    )(*operands)


def _norm_matmul_vmem_bytes(tm, tn, d, has_small):
    need = 2 * tm * d * 4 + 2 * d * tn * 2 + 2 * tm * tn * 2 + tm * d * 2 + tm * tn * 4
    if has_small:
        need += d * DT_PAD * 2 + 2 * tm * DT_PAD * 4
    return need + VMEM_COMPILER_SCRATCH_BYTES


CONV_SLAB = 256
SSD_CHUNKS_PER_STEP = 4
OUT_ROWS = 32
TAIL_ROWS = 16


def _cumsum_rows(v):
    n = v.shape[0]
    row = lax.broadcasted_iota(jnp.int32, v.shape, 0)
    shift = 1
    while shift < n:
        v = v + jnp.where(row >= shift, pltpu.roll(v, shift, axis=0), 0.0)
        shift *= 2
    return v


def _split_bf16(v):
    hi = v.astype(BF16)
    lo = (v - hi.astype(F32)).astype(BF16)
    return jnp.concatenate([hi, lo], axis=1)


def _ssd_chunk(xbc_ref, z_ref, dtraw_ref, convw_ref, convb_ref, dtb_ref, alog_ref, dskip_ref,
               ng_ref, e_ref, shift_ref, y_ref, tail_ref, xs_ref, bc_ref, state_ref):
    q = SSM_CHUNK

    shift_mat = shift_ref[...]

    def conv_slab(s):
        c0 = s * CONV_SLAB
        cols = slice(c0, c0 + CONV_SLAB)
        u_b = xbc_ref[:, cols]
        hist = jnp.concatenate([tail_ref[:, cols], u_b], axis=0)
        w = convw_ref[:, cols]
        w_b = w.astype(BF16)
        weighted = [hist * w_b[SSM_CONV - 1 - j:SSM_CONV - j, :] for j in range(1, SSM_CONV)]
        pad_rows = shift_mat.shape[1] - (SSM_CONV - 1) * (TAIL_ROWS + q)
        weighted.append(jnp.zeros((pad_rows, CONV_SLAB), BF16))
        acc = (convb_ref[:, cols] + w[SSM_CONV - 1:SSM_CONV, :] * u_b.astype(F32)
               + _dot(shift_mat, jnp.concatenate(weighted, axis=0)))
        tail_ref[:, cols] = u_b[q - TAIL_ROWS:q, :]
        act = _silu(acc)
        if c0 < SSM_D_INNER:
            xs_ref[:, cols] = act
        else:
            bc_ref[:, c0 - SSM_D_INNER:c0 - SSM_D_INNER + CONV_SLAB] = act.astype(BF16)

    pre = dtraw_ref[...] + dtb_ref[...]
    dt = jnp.maximum(pre, 0.0) + jnp.log(1.0 + jnp.exp(-jnp.abs(pre)))
    a = -jnp.exp(alog_ref[...]) * LOG2E
    acs = _cumsum_rows(dt * a)
    acs_t = acs.T
    eacs = jnp.exp2(acs)
    dec = jnp.exp2(acs[q - 1:q, :] - acs)
    head_vals = jnp.concatenate([_split_bf16(dt), _split_bf16(eacs), _split_bf16(dec)], axis=0)

    row = lax.broadcasted_iota(jnp.int32, (q, q), 0)
    col = lax.broadcasted_iota(jnp.int32, (q, q), 1)
    causal = row >= col
    lane_head = lax.broadcasted_iota(jnp.int32, (q, SSM_GROUP_WIDTH), 1) // SSM_HEAD_DIM

    def group(g):
        lanes = slice(g * SSM_GROUP_WIDTH, (g + 1) * SSM_GROUP_WIDTH)
        e_g = e_ref[:, lanes]
        ex = _dot(head_vals, e_g)
        dt_e, eacs_e, dec_e = ex[0:q, :], ex[q:2 * q, :], ex[2 * q:3 * q, :]

        xs_g = xs_ref[:, lanes]
        xdt = xs_g * dt_e
        xdt_b = xdt.astype(BF16)
        bm_g = bc_ref[:, g * SSM_STATE:(g + 1) * SSM_STATE]
        cm_g = bc_ref[:, SSM_BC_DIM + g * SSM_STATE:SSM_BC_DIM + (g + 1) * SSM_STATE]
        cb = _dot_nt(cm_g, bm_g)

        m_parts = []
        x_parts = []
        for r in range(SSM_HEADS_PER_GROUP):
            h = g * SSM_HEADS_PER_GROUP + r
            seg = acs[:, h:h + 1] - acs_t[h:h + 1, :]
            lmat = jnp.exp2(jnp.where(causal, seg, -jnp.inf))
            m_parts.append((cb * lmat).astype(BF16))
            x_parts.append(jnp.where(lane_head == r, xdt_b, jnp.zeros_like(xdt_b)))
        m_cat = jnp.concatenate(m_parts, axis=1)
        x_bd = jnp.concatenate(x_parts, axis=0)
        y_diag = _dot(m_cat, x_bd)

        st = state_ref[g]
        y_off = _dot(cm_g, st.astype(BF16))
        xw = (xdt * dec_e).astype(BF16)
        state_ref[g] = st * eacs_e[q - 1:q, :] + _dot_tn(bm_g, xw)

        for r0 in range(0, q, OUT_ROWS):
            rr = slice(r0, r0 + OUT_ROWS)
            y = y_diag[rr, :] + y_off[rr, :] * eacs_e[rr, :] + xs_g[rr, :] * dskip_ref[:, lanes]
            y = y * _silu(z_ref[rr, lanes].astype(F32))
            ms = jnp.mean(y * y, axis=-1, keepdims=True)
            y = y * lax.rsqrt(ms + NORM_EPS) * ng_ref[:, lanes]
            y_ref[rr, lanes] = y.astype(y_ref.dtype)

    xs_slabs, bc_slabs = SSM_D_INNER // CONV_SLAB, SSM_BC_DIM // CONV_SLAB
    slab_order = []
    for k in range(max(xs_slabs, bc_slabs)):
        if k < xs_slabs:
            slab_order.append(k)
        if k < bc_slabs:
            slab_order += [xs_slabs + k, xs_slabs + bc_slabs + k]
    done = 0
    for g in range(SSM_GROUPS):
        needed = (g * SSM_GROUP_WIDTH // CONV_SLAB,
                  (SSM_D_INNER + g * SSM_STATE) // CONV_SLAB,
                  (SSM_D_INNER + SSM_BC_DIM + g * SSM_STATE) // CONV_SLAB)
        ready = 1 + max(slab_order.index(s) for s in needed)
        target = min(len(slab_order), max(ready, done) + 1)
        while done < target:
            conv_slab(slab_order[done])
            done += 1
        group(g)
    assert done == len(slab_order)


CONV_SHIFT_K = 512


def _ssd_kernel(xbc_ref, z_ref, dtraw_ref, convw_ref, convb_ref, dtb_ref, alog_ref, dskip_ref,
                ng_ref, e_ref, shift_ref, y_ref, tail_ref, state_ref, *act_refs):
    @pl.when(pl.program_id(1) == 0)
    def _():
        tail_ref[...] = jnp.zeros_like(tail_ref)
        state_ref[...] = jnp.zeros_like(state_ref)

    for c in range(SSD_CHUNKS_PER_STEP):
        rows = pl.ds(c * SSM_CHUNK, SSM_CHUNK)
        _ssd_chunk(xbc_ref.at[rows], z_ref.at[rows], dtraw_ref.at[rows], convw_ref, convb_ref,
                   dtb_ref, alog_ref, dskip_ref, ng_ref, e_ref, shift_ref, y_ref.at[rows],
                   tail_ref, act_refs[2 * c], act_refs[2 * c + 1], state_ref)


def _conv_shift_matrix():
    q = SSM_CHUNK
    m = np.zeros((q, CONV_SHIFT_K), np.float32)
    for j in range(1, SSM_CONV):
        for t in range(q):
            m[t, (j - 1) * (TAIL_ROWS + q) + TAIL_ROWS + t - j] = 1.0
    return m


def _ssd(proj3, dt3, conv_w, conv_b, dt_bias, a_log, d_skip_e, norm_g, expand_mat):
    b, s, _ = proj3.shape
    q = SSD_CHUNKS_PER_STEP * SSM_CHUNK
    full = lambda shape: pl.BlockSpec(shape, lambda bi, ci: (0,) * len(shape))
    return pl.pallas_call(
        _ssd_kernel,
        grid=(b, s // q),
        in_specs=[
            pl.BlockSpec((None, q, SSM_CONV_DIM), lambda bi, ci: (bi, ci, OFF_XBC // SSM_CONV_DIM)),
            pl.BlockSpec((None, q, SSM_D_INNER), lambda bi, ci: (bi, ci, OFF_Z // SSM_D_INNER)),
            pl.BlockSpec((None, q, DT_PAD), lambda bi, ci: (bi, ci, 0)),
            full((SSM_CONV, SSM_CONV_DIM)),
            full((1, SSM_CONV_DIM)),
            full((1, DT_PAD)),
            full((1, DT_PAD)),
            full((1, SSM_D_INNER)),
            full((1, SSM_D_INNER)),
            full((2 * DT_PAD, SSM_D_INNER)),
            full((SSM_CHUNK, CONV_SHIFT_K)),
        ],
        out_specs=pl.BlockSpec((None, q, SSM_D_INNER), lambda bi, ci: (bi, ci, 0)),
        out_shape=jax.ShapeDtypeStruct((b, s, SSM_D_INNER), BF16),
        scratch_shapes=[
            pltpu.VMEM((TAIL_ROWS, SSM_CONV_DIM), BF16),
            pltpu.VMEM((SSM_GROUPS, SSM_STATE, SSM_GROUP_WIDTH), F32),
        ] + [pltpu.VMEM((SSM_CHUNK, SSM_D_INNER), F32),
             pltpu.VMEM((SSM_CHUNK, 2 * SSM_BC_DIM), BF16)] * SSD_CHUNKS_PER_STEP,
        compiler_params=_cparams(("parallel", "arbitrary")),
        name="ssd",
    )(proj3, proj3, dt3, conv_w, conv_b, dt_bias, a_log, d_skip_e, norm_g, expand_mat,
      jnp.asarray(_conv_shift_matrix(), BF16))


def _t5_bucket_static(rel):
    n = np.maximum(rel, 0)
    max_exact = REL_BUCKETS // 2
    nf = np.maximum(n, 1).astype(np.float64)
    large = max_exact + (np.log(nf / max_exact) / math.log(REL_MAX_DIST / max_exact)
                         * (REL_BUCKETS - max_exact)).astype(np.int64)
    large = np.minimum(large, REL_BUCKETS - 1)
    return np.where(n < max_exact, n, large)


def _near_bucket_table():
    kj = np.arange(2 * ATT_BLOCK)[:, None]
    qi = np.arange(ATT_BLOCK)[None, :]
    rel = qi + ATT_BLOCK - kj
    return np.where(rel >= 0, _t5_bucket_static(rel), -1).astype(np.int32)


def _bias_table_kernel(relb_ref, bucket_ref, o_ref):
    h = pl.program_id(0)
    bucket = bucket_ref[...]
    acc = jnp.full(bucket.shape, -jnp.inf, F32)
    for bkt in range(REL_BUCKETS):
        acc = jnp.where(bucket == bkt, relb_ref[bkt, h], acc)
    o_ref[...] = acc * LOG2E


def _bias_table(rel_bias):
    bucket = jnp.asarray(_near_bucket_table())
    return pl.pallas_call(
        _bias_table_kernel,
        grid=(DIFF_HEADS,),
        in_specs=[
            pl.BlockSpec(memory_space=pltpu.SMEM),
            pl.BlockSpec(bucket.shape, lambda h: (0, 0)),
        ],
        out_specs=pl.BlockSpec((None,) + bucket.shape, lambda h: (h, 0, 0)),
        out_shape=jax.ShapeDtypeStruct((DIFF_HEADS,) + bucket.shape, F32),
        compiler_params=_cparams(("arbitrary",)),
        name="bias_table",
    )(rel_bias, bucket)


class _AttnTile:
    def __init__(self, i, far_bias, lam, q_ref, k_ref, g_ref, tab_ref, sg_ref, o_ref, vt_ref, s_refs):
        self.i = i
        self.far_bias, self.lam = far_bias, lam
        self.k_ref, self.g_ref, self.tab_ref, self.sg_ref = k_ref, g_ref, tab_ref, sg_ref
        self.o_ref, self.vt_ref, self.s_refs = o_ref, vt_ref, s_refs
        self.rows = slice(i * ATT_BLOCK, (i + 1) * ATT_BLOCK)
        self.q_ref = q_ref
        self.qc = None
        self.mx = [None, None]
        self.probs = [[], []]

    def _scaled_queries(self):
        if self.qc is None:
            qq = self.q_ref[self.rows, :]
            lane = lax.broadcasted_iota(jnp.int32, qq.shape, 1)
            scale = DIFF_HEAD_DIM ** -0.5 * LOG2E
            self.qc = [jnp.where(keep, qq, jnp.zeros_like(qq)) * scale
                       for keep in (lane < DIFF_HEAD_DIM, lane >= DIFF_HEAD_DIM)]
        return self.qc

    def _near(self, j):
        return j >= self.i - 1

    def _score_step(self, c, j):
        blk = ATT_BLOCK
        st = _dot_nt(self.k_ref[j * blk:(j + 1) * blk, :], self._scaled_queries()[c])
        if j == self.i:
            bm = self._diag_scores(c, j, st)
        else:
            if self._near(j):
                t0 = (j - (self.i - 1)) * blk
                st = st + self.tab_ref[t0:t0 + blk, :]
            self.s_refs[c][j * blk:(j + 1) * blk, :] = st
            bm = jnp.max(st, axis=0, keepdims=True)
            if not self._near(j):
                bm = bm + self.far_bias
        self.mx[c] = bm if self.mx[c] is None else jnp.maximum(self.mx[c], bm)

    def _diag_scores(self, c, j, st):
        blk, half = ATT_BLOCK, ATT_BLOCK // 2
        t0 = (j - (self.i - 1)) * blk
        top = st[0:half, :] + self.tab_ref[t0:t0 + half, :]
        low = st[half:blk, half:blk] + self.tab_ref[t0 + half:t0 + blk, half:blk]
        self.s_refs[c][j * blk:j * blk + half, :] = top
        self.s_refs[c][j * blk + half:(j + 1) * blk, half:blk] = low
        bm_top = jnp.max(top, axis=0, keepdims=True)
        bm_low = jnp.max(low, axis=0, keepdims=True)
        return jnp.concatenate([bm_top[:, 0:half], jnp.maximum(bm_top[:, half:blk], bm_low)], axis=1)

    def _prob_step(self, c, j):
        blk = ATT_BLOCK
        m = self.mx[c] if self._near(j) else self.mx[c] - self.far_bias
        if j == self.i:
            half = blk // 2
            top = jnp.exp2(self.s_refs[c][j * blk:j * blk + half, :] - m)
            low = jnp.exp2(self.s_refs[c][j * blk + half:(j + 1) * blk, half:blk] - m[:, half:blk])
            low = jnp.concatenate([jnp.zeros((half, half), F32), low], axis=1)
            p = jnp.concatenate([top, low], axis=0)
        else:
            p = jnp.exp2(self.s_refs[c][j * blk:(j + 1) * blk, :] - m)
        self.probs[c].append(p.astype(BF16))

    def score_steps(self):
        return [functools.partial(self._score_step, c, j) for j in range(self.i + 1) for c in range(2)]

    def prob_steps(self):
        return [functools.partial(self._prob_step, c, j) for j in range(self.i + 1) for c in range(2)]

    def finish(self):
        n_keys = (self.i + 1) * ATT_BLOCK
        accs = [_dot(self.vt_ref[:, 0:n_keys], jnp.concatenate(p, axis=0)) for p in self.probs]
        parts = [a[0:DIFF_V_DIM, :] * (1.0 / a[DIFF_V_DIM:DIFF_V_DIM + 1, :]) for a in accs]
        ot = parts[0] - self.lam * parts[1]
        ms = jnp.mean(ot * ot, axis=0, keepdims=True)
        ot = ot * lax.rsqrt(ms + NORM_EPS) * self.sg_ref[...] * (1.0 - LAM_INIT)
        gate = _silu(self.g_ref[self.rows, :].astype(F32))
        self.o_ref[self.rows, :] = (ot.T * gate).astype(self.o_ref.dtype)


def _interleave(first, second):
    n1, n2 = len(first), len(second)
    i1 = i2 = 0
    while i1 < n1 or i2 < n2:
        if i2 >= n2 or (i1 < n1 and i1 * n2 <= i2 * n1):
            first[i1]()
            i1 += 1
        else:
            second[i2]()
            i2 += 1


def _diff_attn_kernel(q_ref, k_ref, v_ref, g_ref, tab_ref, lam_ref, sg_ref, o_ref,
                      vt_ref, *score_refs):
    n_tiles = k_ref.shape[0] // ATT_BLOCK

    for j in range(k_ref.shape[0] // DIFF_V_DIM):
        rows = slice(j * DIFF_V_DIM, (j + 1) * DIFF_V_DIM)
        vt_ref[0:DIFF_V_DIM, rows] = v_ref[rows, :].astype(F32).T.astype(BF16)
    vt_ref[DIFF_V_DIM:, :] = jnp.ones((ONES_ROWS, k_ref.shape[0]), BF16)

    far_bias = tab_ref[0:1, :]
    lam_p = lam_ref[...]
    lam = (jnp.exp(jnp.sum(lam_p[0:1, :] * lam_p[1:2, :], axis=-1, keepdims=True))
           - jnp.exp(jnp.sum(lam_p[2:3, :] * lam_p[3:4, :], axis=-1, keepdims=True)) + LAM_INIT)

    tiles = [_AttnTile(t, far_bias, lam, q_ref, k_ref, g_ref, tab_ref, sg_ref, o_ref, vt_ref,
                       score_refs[2 * t:2 * t + 2]) for t in range(n_tiles)]
    for step in tiles[0].score_steps():
        step()
    for t in range(n_tiles):
        nxt = tiles[t + 1].score_steps() if t + 1 < n_tiles else []
        _interleave(tiles[t].prob_steps(), nxt)
        tiles[t].finish()


def _diff_attn(proj3, table, lam_params, subln_g_cols):
    b, s, _ = proj3.shape
    blk = ATT_BLOCK
    hw = DIFF_V_DIM
    head_cols = lambda off: pl.BlockSpec((None, s, hw), lambda bi, h: (bi, 0, off // hw + h))
    score_scratch = [pltpu.VMEM(((t + 1) * blk, blk), F32)
                     for t in range(s // blk) for _ in range(2)]
    return pl.pallas_call(
        _diff_attn_kernel,
        grid=(b, DIFF_HEADS),
        in_specs=[
            head_cols(OFF_DQ), head_cols(OFF_DK), head_cols(OFF_DV), head_cols(OFF_DG),
            pl.BlockSpec((None, 2 * blk, blk), lambda bi, h: (h, 0, 0)),
            pl.BlockSpec((4, DIFF_HEAD_DIM), lambda bi, h: (0, 0)),
            pl.BlockSpec((hw, blk), lambda bi, h: (0, 0)),
        ],
        out_specs=pl.BlockSpec((None, s, hw), lambda bi, h: (bi, 0, h)),
        out_shape=jax.ShapeDtypeStruct((b, s, DIFF_WIDTH), BF16),
        scratch_shapes=[pltpu.VMEM((hw + ONES_ROWS, s), BF16)] + score_scratch,
        compiler_params=_cparams(("parallel", "parallel")),
        name="diff_attn",
    )(proj3, proj3, proj3, proj3, table, lam_params, subln_g_cols)


MEM_Q_TILE = 256


def _mem_attn_kernel(q_ref, g_ref, k_ref, v_ref, o_ref, vt_ref):
    hd = MEM_HEAD_DIM
    m_len = k_ref.shape[0]
    for r in range(m_len // 128):
        for c in range(hd // 128):
            tile = v_ref[r * 128:(r + 1) * 128, c * 128:(c + 1) * 128].astype(F32)
            vt_ref[c * 128:(c + 1) * 128, r * 128:(r + 1) * 128] = tile.T.astype(BF16)
    vt_ref[hd:, :] = jnp.ones((ONES_ROWS, m_len), BF16)
    scale = MEM_HEAD_DIM ** -0.5 * LOG2E
    kk = k_ref[...]
    for t in range(q_ref.shape[0] // MEM_Q_TILE):
        rows = slice(t * MEM_Q_TILE, (t + 1) * MEM_Q_TILE)
        st = _dot_nt(kk, q_ref[rows, :] * scale)
        p = jnp.exp2(st - jnp.max(st, axis=0, keepdims=True))
        acc = _dot(vt_ref[...], p.astype(BF16))
        ot = acc[0:hd, :] * (1.0 / acc[hd:hd + 1, :])
        o_ref[rows, :] = (ot.T * _silu(g_ref[rows, :].astype(F32))).astype(o_ref.dtype)


def _mem_attn(proj3, kv3):
    b, s, _ = proj3.shape
    m = kv3.shape[1]
    hw = MEM_HEAD_DIM
    return pl.pallas_call(
        _mem_attn_kernel,
        grid=(b, MEM_HEADS),
        in_specs=[
            pl.BlockSpec((None, s, hw), lambda bi, h: (bi, 0, OFF_MQ // hw + h)),
            pl.BlockSpec((None, s, hw), lambda bi, h: (bi, 0, OFF_MG // hw + h)),
            pl.BlockSpec((None, m, hw), lambda bi, h: (bi, 0, h)),
            pl.BlockSpec((None, m, hw), lambda bi, h: (bi, 0, MEM_HEADS + h)),
        ],
        out_specs=pl.BlockSpec((None, s, hw), lambda bi, h: (bi, 0, h)),
        out_shape=jax.ShapeDtypeStruct((b, s, MEM_WIDTH), BF16),
        scratch_shapes=[pltpu.VMEM((hw + ONES_ROWS, m), BF16)],
        compiler_params=_cparams(("parallel", "parallel")),
        name="mem_attn",
    )(proj3, proj3, kv3, kv3)


def _merge_kernel(x_ref, ys_ref, yd_ref, ym_ref, gate_ref, ws_ref, wd_ref, wm_ref, wo_ref,
                  fg_ref, o_ref):
    d = D_MODEL
    gate = gate_ref[...].astype(F32)
    merged = (_sigmoid(gate[:, 0:d]) * _dot(ys_ref[...], ws_ref[...])
              + _sigmoid(gate[:, d:2 * d]) * _dot(yd_ref[...], wd_ref[...])
              + _sigmoid(gate[:, 2 * d:3 * d]) * _dot(ym_ref[...], wm_ref[...]))
    xo = x_ref[...] + _dot(merged.astype(BF16), wo_ref[...])
    ms = jnp.mean(xo * xo, axis=-1, keepdims=True)
    o_ref[...] = xo * lax.rsqrt(ms + NORM_EPS) * fg_ref[...]


def _merge(x2d, ys, yd, ym, proj, w_s, w_d, w_m, w_o, final_g, tm):
    t, d = x2d.shape
    gw = N_BRANCHES * D_MODEL
    rows = lambda width, cb=0: pl.BlockSpec((tm, width), lambda i: (i, cb))
    whole = lambda shape: pl.BlockSpec(shape, lambda i: (0, 0), pipeline_mode=pl.Buffered(1))
    return pl.pallas_call(
        _merge_kernel,
        grid=(t // tm,),
        in_specs=[
            rows(d), rows(SSM_D_INNER), rows(DIFF_WIDTH), rows(MEM_WIDTH),
            rows(gw, OFF_GATE // gw),
            whole(w_s.shape), whole(w_d.shape), whole(w_m.shape), whole(w_o.shape),
            whole((1, d)),
        ],
        out_specs=pl.BlockSpec((tm, d), lambda i: (i, 0)),
        out_shape=jax.ShapeDtypeStruct((t, d), F32),
        compiler_params=_cparams(("parallel",)),
        name="merge",
    )(x2d, ys, yd, ym, proj, w_s, w_d, w_m, w_o, final_g)


def _head_expand_matrix():
    e = np.zeros((2 * DT_PAD, SSM_D_INNER), np.float32)
    for h in range(SSM_HEADS):
        e[h, h * SSM_HEAD_DIM:(h + 1) * SSM_HEAD_DIM] = 1.0
        e[DT_PAD + h, h * SSM_HEAD_DIM:(h + 1) * SSM_HEAD_DIM] = 1.0
    return e


def _pad_lanes(v, width):
    return jnp.pad(v.reshape(1, -1), ((0, 0), (0, width - v.shape[-1])))


def kernel(x, mem, norm_gain, w_in, conv_w, conv_b, dt_bias, a_log, d_skip, ssm_norm_gain,
           lambda_q1, lambda_k1, lambda_q2, lambda_k2, subln_gain, mem_norm_gain, w_mem_kv,
           w_br_ssm, w_br_diff, w_br_mem, w_out, rel_bias, final_norm_gain):
    b, s, d = x.shape
    t = b * s
    assert norm_gain.shape[0] == 1, "single-layer (DEPTH == 1) kernel"
    assert s % ATT_BLOCK == 0 and s % (SSD_CHUNKS_PER_STEP * SSM_CHUNK) == 0 and d == D_MODEL
    assert int(_t5_bucket_static(np.array([ATT_BLOCK]))[0]) == REL_BUCKETS - 1

    ro = _REF_OFFS
    wi = w_in[0].astype(BF16)
    w_main = jnp.concatenate([
        wi[:, ro[1]:ro[2]],
        wi[:, ro[0]:ro[1]],
        wi[:, ro[3]:ro[10]],
    ], axis=1)
    w_dt = jnp.pad(wi[:, ro[2]:ro[3]], ((0, 0), (0, DT_PAD - SSM_HEADS)))

    x2d = x.reshape(t, d)
    proj, dt_raw = _norm_matmul(x2d, norm_gain[0].reshape(1, d), w_main, w_dt,
                                min(IN_PROJ_ROWS, t), IN_PROJ_COLS)
    proj3 = proj.reshape(b, s, PROJ_DIM)
    dt3 = dt_raw.reshape(b, s, DT_PAD)

    y_ssm = _ssd(
        proj3, dt3, conv_w[0], conv_b[0].reshape(1, -1),
        _pad_lanes(dt_bias[0], DT_PAD), _pad_lanes(a_log[0], DT_PAD),
        jnp.repeat(d_skip[0], SSM_HEAD_DIM).reshape(1, -1), ssm_norm_gain[0].reshape(1, -1),
        jnp.asarray(_head_expand_matrix(), BF16))

    table = _bias_table(rel_bias)
    lam_params = jnp.stack([lambda_q1[0], lambda_k1[0], lambda_q2[0], lambda_k2[0]])
    subln_cols = jnp.broadcast_to(subln_gain[0].reshape(-1, 1), (DIFF_V_DIM, ATT_BLOCK))
    y_diff = _diff_attn(proj3, table, lam_params, subln_cols)

    m_len = mem.shape[1]
    kv, = _norm_matmul(mem.reshape(b * m_len, d), mem_norm_gain[0].reshape(1, d),
                       w_mem_kv[0].astype(BF16), None, min(1024, b * m_len), 1024)
    y_mem = _mem_attn(proj3, kv.reshape(b, m_len, 2 * MEM_WIDTH))

    out = _merge(x2d, y_ssm.reshape(t, -1), y_diff.reshape(t, -1), y_mem.reshape(t, -1), proj,
                 w_br_ssm[0].astype(BF16), w_br_diff[0].astype(BF16), w_br_mem[0].astype(BF16),
                 w_out[0].astype(BF16), final_norm_gain.reshape(1, d), min(512, t))
    return out.reshape(b, s, d)
```
